```python
import math
import jax, jax.numpy as jnp
from jax import lax
import numpy as np

D_MODEL = 1024
BATCH = 8
SEQ = 4096
DEPTH = 4

GRID_W = 64
HEAD_DIM = 64
A_Q_HEADS = 8
A_KV_HEADS = 2
ROPE_THETA = 10000.0
Q_BLOCK = 128
B_HEADS = 4
WIN_R = 8
WIN_C = 16
C_HEADS = 4
CONV_K = 5
GDN_CHUNK = 64
N_EXPERTS = 32
N_GROUPS = 8
TOP_K = 2
D_EXPERT = 512
DN_ALPHA = (2 * DEPTH) ** 0.25
DN_BETA = (8 * DEPTH) ** -0.25

A_Q_W = A_Q_HEADS * HEAD_DIM
A_KV_W = A_KV_HEADS * HEAD_DIM
B_W = B_HEADS * HEAD_DIM
C_W = C_HEADS * HEAD_DIM
IN_SPLITS = (A_Q_W, A_KV_W, A_KV_W,
             B_W, B_W, B_W,
             C_W, C_W, C_W, C_W,
             C_HEADS, C_HEADS, C_HEADS, C_HEADS,
             3 * D_MODEL)
IN_VALUE_BLOCKS = (2, 5, 8)
D_IN = sum(IN_SPLITS)

kernel_name = "hybrid_gqa_natten_gdn_moe_deepnorm"

F32 = jnp.float32


def _split_cols(t, sizes):
    idx = np.cumsum(sizes)[:-1].tolist()
    return jnp.split(t, idx, axis=-1)


def _layernorm(x, g, b, eps=1e-5):
    xf = x.astype(F32)
    mu = jnp.mean(xf, axis=-1, keepdims=True)
    xc = xf - mu
    var = jnp.mean(xc * xc, axis=-1, keepdims=True)
    return (xc * lax.rsqrt(var + eps) * g.astype(F32) + b.astype(F32)).astype(x.dtype)


def _rms(x, g, eps=1e-6):
    xf = x.astype(F32)
    y = xf * lax.rsqrt(jnp.mean(xf * xf, axis=-1, keepdims=True) + eps)
    return (y * g.astype(F32)).astype(x.dtype)


def _l2n(x, eps=1e-6):
    xf = x.astype(F32)
    return xf * lax.rsqrt(jnp.sum(xf * xf, axis=-1, keepdims=True) + eps)


def _axial_rope_tables(seq):
    t = jnp.arange(seq, dtype=jnp.int32)
    row = (t // GRID_W).astype(F32)
    col = (t % GRID_W).astype(F32)
    half = HEAD_DIM // 2
    inv = ROPE_THETA ** (-jnp.arange(0, half, 2, dtype=F32) / half)
    ang_r = row[:, None] * inv[None, :]
    ang_c = col[:, None] * inv[None, :]
    return (jnp.cos(ang_r), jnp.sin(ang_r), jnp.cos(ang_c), jnp.sin(ang_c))


def _rotate(x, cos, sin):
    x1, x2 = jnp.split(x, 2, axis=-1)
    c = cos[:, None, :]
    s = sin[:, None, :]
    return jnp.concatenate([x1 * c - x2 * s, x2 * c + x1 * s], axis=-1)


def _axial_rope(x, tables):
    cr, sr, cc, sc = tables
    xr, xc = jnp.split(x.astype(F32), 2, axis=-1)
    return jnp.concatenate([_rotate(xr, cr, sr), _rotate(xc, cc, sc)], axis=-1).astype(x.dtype)


def _global_gqa(q, k, v, q_g, k_g, tables):
    B_, S_, _ = q.shape
    q = q.reshape(B_, S_, A_Q_HEADS, HEAD_DIM)
    k = k.reshape(B_, S_, A_KV_HEADS, HEAD_DIM)
    v = v.reshape(B_, S_, A_KV_HEADS, HEAD_DIM)
    q = _axial_rope(_rms(q, q_g), tables) * (HEAD_DIM ** -0.5)
    k = _axial_rope(_rms(k, k_g), tables)
    grp = A_Q_HEADS // A_KV_HEADS
    nb = S_ // Q_BLOCK
    qb = q.reshape(B_, nb, Q_BLOCK, A_KV_HEADS, grp, HEAD_DIM).transpose(1, 0, 2, 3, 4, 5)

    def block(qi):
        s = jnp.einsum('bqhgd,bshd->bhgqs', qi, k).astype(F32)
        p = jax.nn.softmax(s, axis=-1).astype(v.dtype)
        return jnp.einsum('bhgqs,bshd->bqhgd', p, v)

    o = lax.map(block, qb)
    return o.transpose(1, 0, 2, 3, 4, 5).reshape(B_, S_, A_Q_W)


def _neighbourhood_attn(q, k, v, rpb):
    B_, S_, _ = q.shape
    rows = S_ // GRID_W
    wr = min(WIN_R, rows)
    shp = (B_, rows, GRID_W, B_HEADS, HEAD_DIM)
    q = q.reshape(shp) * (HEAD_DIM ** -0.5)
    k = k.reshape(shp)
    v = v.reshape(shp)
    c = jnp.arange(GRID_W, dtype=jnp.int32)
    col_idx = jnp.clip(c - WIN_C // 2, 0, GRID_W - WIN_C)[:, None] + jnp.arange(WIN_C, dtype=jnp.int32)[None, :]
    dc = col_idx - c[:, None] + (WIN_C - 1)
    rpb_c = rpb[:, :, dc]

    def row_block(args):
        q_r, r = args
        r0 = jnp.clip(r - wr // 2, 0, rows - wr)
        k_rows = lax.dynamic_slice_in_dim(k, r0, wr, axis=1)
        v_rows = lax.dynamic_slice_in_dim(v, r0, wr, axis=1)
        k_win = k_rows[:, :, col_idx]
        v_win = v_rows[:, :, col_idx]
        dr = r0 + jnp.arange(wr, dtype=jnp.int32) - r + (WIN_R - 1)
        bias = rpb_c[:, dr].transpose(0, 2, 1, 3)
        s = jnp.einsum('bchd,brcjhd->bhcrj', q_r, k_win).astype(F32) + bias.astype(F32)
        p = jax.nn.softmax(s, axis=(-2, -1)).astype(v.dtype)
        return jnp.einsum('bhcrj,brcjhd->bchd', p, v_win)

    q_rows = q.transpose(1, 0, 2, 3, 4)
    o = lax.map(row_block, (q_rows, jnp.arange(rows, dtype=jnp.int32)))
    return o.transpose(1, 0, 2, 3, 4).reshape(B_, S_, B_W)


def _short_conv(x, w):
    return lax.conv_general_dilated(
        x, w[:, None, :].astype(x.dtype), window_strides=(1,),
        padding=[(CONV_K // 2, CONV_K // 2)],
        dimension_numbers=('NWC', 'WIO', 'NWC'),
        feature_group_count=x.shape[-1])


def _gated_delta_chunked(q, k, v, g, beta):
    B_, S_, H, dk = q.shape
    dv = v.shape[-1]
    C = GDN_CHUNK
    N = S_ // C

    def chunks(t):
        return t.reshape(B_, N, C, H, -1).transpose(1, 0, 3, 2, 4)

    qc = chunks(q * (dk ** -0.5))
    kc = chunks(k)
    vc = chunks(v)
    gc = jnp.cumsum(chunks(g[..., None])[..., 0], axis=-1)
    bc = chunks(beta[..., None])
    incl = jnp.tril(jnp.ones((C, C), dtype=bool))
    strict = jnp.tril(jnp.ones((C, C), dtype=bool), -1)
    decay = jnp.exp(jnp.where(incl, gc[..., :, None] - gc[..., None, :], -jnp.inf))
    k_beta = kc * bc
    m = jnp.where(strict, jnp.einsum('nbhid,nbhjd->nbhij', k_beta, kc) * decay, 0.0)
    rhs = jnp.concatenate([vc * bc, k_beta * jnp.exp(gc)[..., None]], axis=-1)
    sol = lax.linalg.triangular_solve(m + jnp.eye(C, dtype=m.dtype), rhs,
                                      left_side=True, lower=True, unit_diagonal=True)
    u, w = sol[..., :dv], sol[..., dv:]
    qk = jnp.einsum('nbhid,nbhjd->nbhij', qc, kc) * decay

    def step(state, xs):
        q_i, k_i, u_i, w_i, g_i, qk_i = xs
        v_new = u_i - jnp.einsum('bhck,bhkv->bhcv', w_i, state)
        o = (jnp.einsum('bhck,bhkv->bhcv', q_i * jnp.exp(g_i)[..., None], state)
             + jnp.einsum('bhij,bhjv->bhiv', qk_i, v_new))
        g_last = g_i[..., -1:]
        state = (state * jnp.exp(g_last)[..., None]
                 + jnp.einsum('bhck,bhcv->bhkv', k_i * jnp.exp(g_last - g_i)[..., None], v_new))
        return state, o

    s0 = jnp.zeros((B_, H, dk, dv), F32)
    _, o = lax.scan(step, s0, (qc, kc, u, w, gc, qk))
    return o.transpose(1, 0, 3, 2, 4).reshape(B_, S_, H, dv)


def _gdn_bidir(q, k, v, z, b_f, b_b, a_f, a_b, conv_w, A_log, dt_bias, norm_g):
    B_, S_, _ = q.shape
    qkv = jax.nn.silu(_short_conv(jnp.concatenate([q, k, v], axis=-1), conv_w))
    q, k, v = jnp.split(qkv, 3, axis=-1)
    hs = (B_, S_, C_HEADS, HEAD_DIM)
    q = _l2n(q.reshape(hs))
    k = _l2n(k.reshape(hs))
    v = v.reshape(hs).astype(F32)

    def gates(bb, aa, d):
        beta = jax.nn.sigmoid(bb.astype(F32))
        g = -jnp.exp(A_log[d].astype(F32)) * jax.nn.softplus(aa.astype(F32) + dt_bias[d].astype(F32))
        return g, beta

    g_f, beta_f = gates(b_f, a_f, 0)
    g_b, beta_b = gates(b_b, a_b, 1)
    o_f = _gated_delta_chunked(q, k, v, g_f, beta_f)
    flip = lambda t: jnp.flip(t, axis=1)
    o_b = flip(_gated_delta_chunked(flip(q), flip(k), flip(v), flip(g_b), flip(beta_b)))
    o = _rms(o_f + o_b, norm_g) * jax.nn.silu(z.reshape(hs).astype(F32))
    return o.reshape(B_, S_, C_W).astype(z.dtype)


def _mixer(h, w_in, q_g, k_g, rpb, conv_w, A_log, dt_bias, gdn_g, wa, wb, wc, w_out, tables):
    p = h @ w_in
    (aq, ak, av, bq, bk, bv, cq, ck, cv, cz, cbf, cbb, caf, cab, gate) = _split_cols(p, IN_SPLITS)
    ya = _global_gqa(aq, ak, av, q_g, k_g, tables)
    yb = _neighbourhood_attn(bq, bk, bv, rpb)
    yc = _gdn_bidir(cq, ck, cv, cz, cbf, cbb, caf, cab, conv_w, A_log, dt_bias, gdn_g)
    ga, gb, gc = jnp.split(jax.nn.sigmoid(gate.astype(F32)).astype(h.dtype), 3, axis=-1)
    mix = ga * (ya @ wa) + gb * (yb @ wb) + gc * (yc @ wc)
    return mix @ w_out


def _moe(h, w_router, router_bias, w1, w3, w2):
    B_, S_, D = h.shape
    xt = h.reshape(B_ * S_, D)
    scores = jax.nn.sigmoid((xt @ w_router).astype(F32))
    sel = scores + router_bias.astype(F32)
    grp = sel.reshape(-1, N_GROUPS, N_EXPERTS // N_GROUPS)
    grp_score = jnp.sum(lax.top_k(grp, TOP_K)[0], axis=-1)
    best = jnp.argmax(grp_score, axis=-1)
    in_group = (jnp.arange(N_GROUPS)[None, :] == best[:, None])[:, :, None]
    masked = jnp.where(in_group, grp, -jnp.inf).reshape(-1, N_EXPERTS)
    _, eidx = lax.top_k(masked, TOP_K)
    w_sel = jnp.take_along_axis(scores, eidx, axis=-1)
    w_sel = w_sel / jnp.sum(w_sel, axis=-1, keepdims=True)
    comb = jnp.sum(jax.nn.one_hot(eidx, N_EXPERTS, dtype=F32) * w_sel[..., None], axis=1).astype(h.dtype)
    y = jnp.zeros_like(xt)
    for e in range(N_EXPERTS):
        hid = jax.nn.silu(xt @ w1[e]) * (xt @ w3[e])
        y = y + comb[:, e:e + 1] * (hid @ w2[e])
    return y.reshape(B_, S_, D)


def setup_inputs(seed: int = 0) -> dict:
    key = jax.random.key(seed)
    ks = jax.random.split(key, 32)
    D = D_MODEL

    def nrm(k, shape, scale):
        return jax.random.normal(k, shape, F32) * scale

    col_scale = jnp.concatenate([
        jnp.full((n,), DN_BETA if i in IN_VALUE_BLOCKS else 1.0, F32)
        for i, n in enumerate(IN_SPLITS)]) * (D ** -0.5)
    dt = jnp.exp(jax.random.uniform(ks[8], (DEPTH, 2, C_HEADS), F32, math.log(1e-3), math.log(1e-1)))
    return {
        "x": nrm(ks[0], (BATCH, SEQ, D), 1.0),
        "ln0_g": 1.0 + nrm(ks[1], (D,), 0.01),
        "ln0_b": nrm(ks[2], (D,), 0.01),
        "w_in": jax.random.normal(ks[3], (DEPTH, D, D_IN), F32) * col_scale,
        "q_norm_g": 1.0 + nrm(ks[4], (DEPTH, HEAD_DIM), 0.01),
        "k_norm_g": 1.0 + nrm(ks[5], (DEPTH, HEAD_DIM), 0.01),
        "na_rpb": nrm(ks[6], (DEPTH, B_HEADS, 2 * WIN_R - 1, 2 * WIN_C - 1), 0.02),
        "conv_w": nrm(ks[7], (DEPTH, CONV_K, 3 * C_W), CONV_K ** -0.5),
        "A_log": jnp.log(jax.random.uniform(ks[9], (DEPTH, 2, C_HEADS), F32, 1.0, 16.0)),
        "dt_bias": dt + jnp.log(-jnp.expm1(-dt)),
        "gdn_norm_g": 1.0 + nrm(ks[10], (DEPTH, HEAD_DIM), 0.01),
        "w_branch_a": nrm(ks[11], (DEPTH, A_Q_W, D), A_Q_W ** -0.5),
        "w_branch_b": nrm(ks[12], (DEPTH, B_W, D), B_W ** -0.5),
        "w_branch_c": nrm(ks[13], (DEPTH, C_W, D), C_W ** -0.5),
        "w_out": nrm(ks[14], (DEPTH, D, D), DN_BETA * D ** -0.5),
        "ln1_g": 1.0 + nrm(ks[15], (DEPTH, D), 0.01),
        "ln1_b": nrm(ks[16], (DEPTH, D), 0.01),
        "w_router": nrm(ks[17], (D, N_EXPERTS), D ** -0.5),
        "router_bias": nrm(ks[18], (N_EXPERTS,), 0.01),
        "w1": nrm(ks[19], (DEPTH, N_EXPERTS, D, D_EXPERT), D ** -0.5),
        "w3": nrm(ks[20], (DEPTH, N_EXPERTS, D, D_EXPERT), DN_BETA * D ** -0.5),
        "w2": nrm(ks[21], (DEPTH, N_EXPERTS, D_EXPERT, D), DN_BETA * D_EXPERT ** -0.5),
        "ln2_g": 1.0 + nrm(ks[22], (DEPTH, D), 0.01),
        "ln2_b": nrm(ks[23], (DEPTH, D), 0.01),
    }


def reference(x, ln0_g, ln0_b, w_in, q_norm_g, k_norm_g, na_rpb, conv_w, A_log, dt_bias,
              gdn_norm_g, w_branch_a, w_branch_b, w_branch_c, w_out, ln1_g, ln1_b,
              w_router, router_bias, w1, w3, w2, ln2_g, ln2_b):
    tables = _axial_rope_tables(x.shape[1])
    h = _layernorm(x, ln0_g, ln0_b)
    for l in range(DEPTH):
        mix = _mixer(h, w_in[l], q_norm_g[l], k_norm_g[l], na_rpb[l], conv_w[l], A_log[l], dt_bias[l],
                     gdn_norm_g[l], w_branch_a[l], w_branch_b[l], w_branch_c[l], w_out[l], tables)
        h = _layernorm(DN_ALPHA * h + mix, ln1_g[l], ln1_b[l])
        ffn = _moe(h, w_router, router_bias, w1[l], w3[l], w2[l])
        h = _layernorm(DN_ALPHA * h + ffn, ln2_g[l], ln2_b[l])
    return h
```

```python
import functools
import math

import numpy as np
import jax
import jax.numpy as jnp
from jax import lax
from jax.experimental import pallas as pl
from jax.experimental.pallas import tpu as pltpu

F32 = jnp.float32
BF16 = jnp.bfloat16

D_MODEL = 1024
GRID_W = 64
HEAD_DIM = 64
A_Q_HEADS = 8
A_KV_HEADS = 2
ROPE_THETA = 10000.0
B_HEADS = 4
WIN_R = 8
WIN_C = 16
C_HEADS = 4
CONV_K = 5
N_EXPERTS = 32
N_GROUPS = 8
GROUP_SIZE = N_EXPERTS // N_GROUPS
TOP_K = 2
D_EXPERT = 512
DEPTH = 4
DN_ALPHA = (2 * DEPTH) ** 0.25

A_Q_W = A_Q_HEADS * HEAD_DIM
A_KV_W = A_KV_HEADS * HEAD_DIM
B_W = B_HEADS * HEAD_DIM
C_W = C_HEADS * HEAD_DIM
MAIN_W = A_Q_W + 2 * A_KV_W + 3 * B_W + 4 * C_W
SMALL_W = 4 * C_HEADS

LANES = 128
TOKEN_TILE = 512
ATTN_Q_TILE = 256
NA_ROWS_PER_BLOCK = 8
GDN_CHUNK = 64
GDN_BLOCK = 512
EXPERT_TILE = 256
VMEM_LIMIT = 48 * 1024 * 1024
NEG_BIG = -1e30
HIGHEST = lax.Precision.HIGHEST


def _cparams(*sem):
    return pltpu.CompilerParams(dimension_semantics=sem, vmem_limit_bytes=VMEM_LIMIT)


def _dot(a, b):
    return jnp.dot(a, b, preferred_element_type=F32)


def _dot_nt(a, b):
    return lax.dot_general(a, b, (((1,), (1,)), ((), ())), preferred_element_type=F32)


def _dot_tn(a, b):
    return lax.dot_general(a, b, (((0,), (0,)), ((), ())), preferred_element_type=F32)


def _dot_exact(a, b):
    return jnp.dot(a, b, preferred_element_type=F32, precision=HIGHEST)


def _seg_sum(y, seg):
    hi = y.astype(BF16)
    lo = (y - hi.astype(F32)).astype(BF16)
    return _dot(hi, seg) + _dot(lo, seg)


def _sigmoid(x):
    return 1.0 / (1.0 + jnp.exp(-x))


def _silu(x):
    return x * _sigmoid(x)


def _layernorm_rows(y, g, b):
    mu = jnp.mean(y, axis=-1, keepdims=True)
    yc = y - mu
    var = jnp.mean(yc * yc, axis=-1, keepdims=True)
    return yc * lax.rsqrt(var + 1e-5) * g + b


def _ln_kernel(x_ref, g_ref, b_ref, o_ref):
    o_ref[...] = _layernorm_rows(x_ref[...], g_ref[...], b_ref[...])


def _input_layernorm(x, g, b):
    t, d = x.shape
    row = pl.BlockSpec((TOKEN_TILE, d), lambda i: (i, 0))
    vec = pl.BlockSpec((1, d), lambda i: (0, 0))
    return pl.pallas_call(
        _ln_kernel, grid=(t // TOKEN_TILE,), in_specs=[row, vec, vec], out_specs=row,
        out_shape=jax.ShapeDtypeStruct((t, d), F32), compiler_params=_cparams("parallel"),
        name="input_layernorm")(x, g.reshape(1, d), b.reshape(1, d))


def _proj_kernel(h_ref, w_ref, ws_ref, cos_ref, sin_ref, qg_ref, kg_ref, seg_ref, gp_ref,
                 q_ref, k_ref, v_ref, bq_ref, bk_ref, bv_ref, c_ref, z_ref, sm_ref):
    x = h_ref[...].astype(BF16)
    tm = x.shape[0]
    lane = lax.broadcasted_iota(jnp.int32, (tm, LANES), 1)
    first_half = (lane % (HEAD_DIM // 2)) < (HEAD_DIM // 4)
    low_head = lane < HEAD_DIM
    cos = cos_ref[...]
    sin = sin_ref[...]
    seg = seg_ref[...]

    def mm(a, b):
        return _dot(x, w_ref[:, a:b])

    def norm_rope(y, g):
        ms = _seg_sum(y * y, seg) * (1.0 / HEAD_DIM)
        yn = y * lax.rsqrt(ms + 1e-6) * g
        partner = jnp.where(first_half, pltpu.roll(yn, LANES - HEAD_DIM // 4, 1),
                            pltpu.roll(yn, HEAD_DIM // 4, 1))
        return yn * cos + partner * sin

    for j in range(A_Q_W // LANES):
        y = norm_rope(mm(j * LANES, (j + 1) * LANES), qg_ref[...]) * (HEAD_DIM ** -0.5)
        yr = pltpu.roll(y, HEAD_DIM, 1)
        kv_head = (2 * j) // (A_Q_HEADS // A_KV_HEADS)
        if kv_head == 0:
            even, odd = jnp.where(low_head, y, 0.0), jnp.where(low_head, yr, 0.0)
        else:
            even, odd = jnp.where(low_head, 0.0, yr), jnp.where(low_head, 0.0, y)
        q_ref[:, (2 * j) * LANES:(2 * j + 1) * LANES] = even.astype(BF16)
        q_ref[:, (2 * j + 1) * LANES:(2 * j + 2) * LANES] = odd.astype(BF16)
    o = A_Q_W
    k_ref[...] = norm_rope(mm(o, o + A_KV_W), kg_ref[...]).astype(BF16)
    o += A_KV_W
    v_ref[...] = mm(o, o + A_KV_W).astype(BF16)
    o += A_KV_W
    bq_ref[...] = (mm(o, o + B_W) * (HEAD_DIM ** -0.5)).astype(BF16)
    o += B_W
    bk_ref[...] = mm(o, o + B_W).astype(BF16)
    o += B_W
    bv_ref[...] = mm(o, o + B_W).astype(BF16)
    o += B_W
    c_ref[...] = mm(o, o + 3 * C_W)
    o += 3 * C_W
    z_ref[...] = mm(o, o + C_W).astype(BF16)

    s = _dot(x, ws_ref[...])
    a = jnp.exp(gp_ref[0:1, :])
    u = s + gp_ref[1:2, :]
    softplus = jnp.maximum(u, 0.0) + jnp.log(1.0 + jnp.exp(-jnp.abs(u)))
    sm_ref[...] = jnp.where(lane < 2 * C_HEADS, _sigmoid(s), jnp.where(lane < SMALL_W, -a * softplus, 0.0))


def _projection(h, w_main, w_small, cos, sin, qg, kg, seg, gp, seq):
    t, d = h.shape
    tm = TOKEN_TILE
    nseq = seq // tm

    def rows(w):
        return pl.BlockSpec((tm, w), lambda i: (i, 0))

    def const(shape):
        return pl.BlockSpec(shape, lambda i: (0, 0))

    pos = pl.BlockSpec((tm, LANES), lambda i: (i % nseq, 0))
    out_w = [(2 * A_Q_W, BF16), (A_KV_W, BF16), (A_KV_W, BF16), (B_W, BF16), (B_W, BF16), (B_W, BF16),
             (3 * C_W, F32), (C_W, BF16), (LANES, F32)]
    return pl.pallas_call(
        _proj_kernel, grid=(t // tm,),
        in_specs=[rows(d), const((d, MAIN_W)), const((d, LANES)), pos, pos, const((1, LANES)),
                  const((1, LANES)), const((LANES, LANES)), const((8, LANES))],
        out_specs=[rows(w) for w, _ in out_w],
        out_shape=[jax.ShapeDtypeStruct((t, w), dt) for w, dt in out_w],
        compiler_params=_cparams("parallel"), name="projection",
    )(h, w_main, w_small, cos, sin, qg, kg, seg, gp)


def _gqa_kernel(q_ref, k_ref, v_ref, o_ref):
    tq = q_ref.shape[0]
    lane = lax.broadcasted_iota(jnp.int32, (tq, LANES), 1)
    low_head = lane < HEAD_DIM
    k = k_ref[...]
    v = v_ref[...]
    group = A_Q_HEADS // A_KV_HEADS
    outs = []
    for h in range(A_Q_HEADS):
        s = _dot_nt(q_ref[:, h * LANES:(h + 1) * LANES], k)
        m = jnp.max(s, axis=-1, keepdims=True)
        p = jnp.exp(s - m)
        l = jnp.sum(p, axis=-1, keepdims=True)
        outs.append(_dot(p.astype(BF16), v) / l)
    for j in range(A_Q_HEADS // 2):
        even, odd = outs[2 * j], outs[2 * j + 1]
        if (2 * j) // group == 0:
            blk = jnp.where(low_head, even, pltpu.roll(odd, HEAD_DIM, 1))
        else:
            blk = jnp.where(low_head, pltpu.roll(even, HEAD_DIM, 1), odd)
        o_ref[:, j * LANES:(j + 1) * LANES] = blk.astype(BF16)


def _gqa(qpad, k, v, batch, seq):
    t = qpad.shape[0]
    tq = ATTN_Q_TILE
    nq = seq // tq
    kv = pl.BlockSpec((seq, A_KV_W), lambda b, i: (b, 0))
    return pl.pallas_call(
        _gqa_kernel, grid=(batch, nq),
        in_specs=[pl.BlockSpec((tq, 2 * A_Q_W), lambda b, i: (b * nq + i, 0)), kv, kv],
        out_specs=pl.BlockSpec((tq, A_Q_W), lambda b, i: (b * nq + i, 0)),
        out_shape=jax.ShapeDtypeStruct((t, A_Q_W), BF16),
        compiler_params=_cparams("parallel", "parallel"), name="gqa_attention",
    )(qpad, k, v)


def _na_key_row0(qb, rows):
    return min(max(NA_ROWS_PER_BLOCK * qb - WIN_R // 2, 0), rows - 2 * NA_ROWS_PER_BLOCK)


def _na_kernel(q_ref, k_ref, v_ref, bias_ref, o_ref, *, rows):
    qb = pl.program_id(0)
    nq = q_ref.shape[0]
    nk = 2 * NA_ROWS_PER_BLOCK * GRID_W
    row0 = jnp.clip(NA_ROWS_PER_BLOCK * qb - WIN_R // 2, 0, rows - 2 * NA_ROWS_PER_BLOCK)
    start = pl.multiple_of(row0 * GRID_W, 4 * GRID_W)
    lane = lax.broadcasted_iota(jnp.int32, (nq, LANES), 1)
    low_head = lane < HEAD_DIM
    zero = jnp.zeros((), BF16)
    for j in range(B_W // LANES):
        kblk = k_ref[pl.ds(start, nk), j * LANES:(j + 1) * LANES]
        vblk = v_ref[pl.ds(start, nk), j * LANES:(j + 1) * LANES]
        qblk = q_ref[:, j * LANES:(j + 1) * LANES]
        outs = []
        for par in range(2):
            qh = jnp.where(low_head if par == 0 else jnp.logical_not(low_head), qblk, zero)
            s = _dot_nt(qh, kblk) + bias_ref[0, 2 * j + par]
            m = jnp.max(s, axis=-1, keepdims=True)
            p = jnp.exp(s - m)
            l = jnp.sum(p, axis=-1, keepdims=True)
            outs.append(_dot(p.astype(BF16), vblk) / l)
        o_ref[:, j * LANES:(j + 1) * LANES] = jnp.where(low_head, outs[0], outs[1]).astype(BF16)


def _na_bias_table(rpb, rows):
    nqb = rows // NA_ROWS_PER_BLOCK
    qi = np.arange(NA_ROWS_PER_BLOCK * GRID_W)
    ki = np.arange(2 * NA_ROWS_PER_BLOCK * GRID_W)
    wr = min(WIN_R, rows)
    dr = np.zeros((nqb, qi.size, ki.size), np.int32)
    dc = np.zeros_like(dr)
    ok = np.zeros(dr.shape, bool)
    for qb in range(nqb):
        r = NA_ROWS_PER_BLOCK * qb + qi // GRID_W
        c = qi % GRID_W
        kr = _na_key_row0(qb, rows) + ki // GRID_W
        kc = ki % GRID_W
        r0 = np.clip(r - wr // 2, 0, rows - wr)
        c0 = np.clip(c - WIN_C // 2, 0, GRID_W - WIN_C)
        in_r = (kr[None, :] >= r0[:, None]) & (kr[None, :] < r0[:, None] + wr)
        in_c = (kc[None, :] >= c0[:, None]) & (kc[None, :] < c0[:, None] + WIN_C)
        ok[qb] = in_r & in_c
        dr[qb] = np.clip(kr[None, :] - r[:, None] + (WIN_R - 1), 0, 2 * WIN_R - 2)
        dc[qb] = np.clip(kc[None, :] - c[:, None] + (WIN_C - 1), 0, 2 * WIN_C - 2)
    flat = jnp.asarray(dr * (2 * WIN_C - 1) + dc)
    table = jnp.take(rpb.reshape(B_HEADS, -1), flat.reshape(-1), axis=1).reshape((B_HEADS,) + dr.shape)
    table = jnp.where(jnp.asarray(ok)[None], table, NEG_BIG)
    return jnp.transpose(table, (1, 0, 2, 3))


def _neighbourhood(q, k, v, bias, batch, seq):
    t = q.shape[0]
    rows = seq // GRID_W
    nqb = rows // NA_ROWS_PER_BLOCK
    nq = NA_ROWS_PER_BLOCK * GRID_W
    kv = pl.BlockSpec((seq, B_W), lambda qb, b: (b, 0))
    qo = pl.BlockSpec((nq, B_W), lambda qb, b: (b * nqb + qb, 0))
    return pl.pallas_call(
        functools.partial(_na_kernel, rows=rows), grid=(nqb, batch),
        in_specs=[qo, kv, kv, pl.BlockSpec((1, B_HEADS, nq, 2 * nq), lambda qb, b: (qb, 0, 0, 0))],
        out_specs=qo, out_shape=jax.ShapeDtypeStruct((t, B_W), BF16),
        compiler_params=_cparams("parallel", "parallel"), name="neighbourhood_attention",
    )(q, k, v, bias)


def _conv_kernel(c_ref, prev_ref, next_ref, w_ref, seg_ref, q_ref, k_ref, v_ref, pad_ref, *, tiles_per_seq):
    i = pl.program_id(0)
    tm = c_ref.shape[0]
    halo = prev_ref.shape[0]
    first = (i % tiles_per_seq) == 0
    last = (i % tiles_per_seq) == tiles_per_seq - 1
    pad_ref[0:halo, :] = jnp.where(first, 0.0, prev_ref[...])
    pad_ref[halo:halo + tm, :] = c_ref[...]
    pad_ref[halo + tm:2 * halo + tm, :] = jnp.where(last, 0.0, next_ref[...])
    acc = None
    for j in range(CONV_K):
        o = halo - CONV_K // 2 + j
        term = w_ref[j:j + 1, :] * pad_ref[o:o + tm, :]
        acc = term if acc is None else acc + term
    y = _silu(acc)
    seg = seg_ref[...]

    def l2n(u):
        return u * lax.rsqrt(_seg_sum(u * u, seg) + 1e-6)

    q_ref[...] = (l2n(y[:, 0:C_W]) * (HEAD_DIM ** -0.5)).astype(BF16)
    k_ref[...] = l2n(y[:, C_W:2 * C_W]).astype(BF16)
    v_ref[...] = y[:, 2 * C_W:3 * C_W].astype(BF16)


def _short_conv_norm(c, conv_w, seg, seq):
    t, w = c.shape
    tm = TOKEN_TILE
    halo = 8
    nb = tm // halo
    last_blk = t // halo - 1
    out = pl.BlockSpec((tm, C_W), lambda i: (i, 0))
    return pl.pallas_call(
        functools.partial(_conv_kernel, tiles_per_seq=seq // tm), grid=(t // tm,),
        in_specs=[pl.BlockSpec((tm, w), lambda i: (i, 0)),
                  pl.BlockSpec((halo, w), lambda i: (jnp.maximum(i * nb - 1, 0), 0)),
                  pl.BlockSpec((halo, w), lambda i: (jnp.minimum((i + 1) * nb, last_blk), 0)),
                  pl.BlockSpec((8, w), lambda i: (0, 0)),
                  pl.BlockSpec((C_W, C_W), lambda i: (0, 0))],
        out_specs=[out, out, out],
        out_shape=[jax.ShapeDtypeStruct((t, C_W), BF16)] * 3,
        scratch_shapes=[pltpu.VMEM((tm + 2 * halo, w), F32)],
        compiler_params=_cparams("parallel"), name="gdn_conv_norm",
    )(c, c, c, conv_w, seg)


def _block_diag(x, same_head):
    return jnp.where(same_head, jnp.concatenate([x] * C_HEADS, axis=0), 0.0)


def _gdn_chunk(c, q_ref, k_ref, v_ref, sm_ref, grow_ref, state_ref, o_ref, e_beta, e_g, tri, reverse):
    C = GDN_CHUNK
    sl = pl.ds(pl.multiple_of(c * C, C), C)
    q = q_ref[sl, :].astype(F32)
    k = k_ref[sl, :].astype(F32)
    v = v_ref[sl, :].astype(F32)
    sm = sm_ref[sl, :]
    beta = _dot_exact(sm, e_beta)
    gc = _dot_exact(tri, _dot_exact(sm, e_g))
    grow = grow_ref[pl.ds(c, 1), :]

    ri = lax.broadcasted_iota(jnp.int32, (C, C_W), 0)
    lj = lax.broadcasted_iota(jnp.int32, (C, C_W), 1) % C
    incl = (ri <= lj) if reverse else (ri >= lj)
    strict = (ri < lj) if reverse else (ri > lj)
    rr = lax.broadcasted_iota(jnp.int32, (C_HEADS * C, C_W), 0) // C
    ll = lax.broadcasted_iota(jnp.int32, (C_HEADS * C, C_W), 1) // HEAD_DIM
    same_head = rr == ll

    def bd(x):
        return _block_diag(x, same_head).astype(BF16)

    decay = jnp.exp(jnp.where(incl, gc - grow, -jnp.inf))
    kb = k * beta
    r = _dot_nt(jnp.concatenate([kb, q], axis=0).astype(BF16), bd(k))
    m = jnp.where(strict, r[:C] * decay, 0.0)
    qk = r[C:] * decay

    p = jnp.where(ri == lj, 1.0, 0.0) - m
    a = _dot(m.astype(BF16), bd(m))
    for _ in range(int(math.log2(C)) - 2):
        r2 = _dot(jnp.concatenate([a, p], axis=0).astype(BF16), bd(a))
        a, p = r2[:C], p + r2[C:]
    t_inv = p + _dot(p.astype(BF16), bd(a))

    egc = jnp.exp(gc)
    rhs = jnp.concatenate([bd(v * beta), bd(kb * egc)], axis=1)
    uw = _dot(t_inv.astype(BF16), rhs)
    u, w = uw[:, :C_W], uw[:, C_W:]

    state = state_ref[...]
    state_b = state.astype(BF16)
    v_new = u - _dot(w.astype(BF16), state_b)
    o = _dot((q * egc).astype(BF16), state_b) + _dot(qk.astype(BF16), bd(v_new))
    g_last = gc[0:1, :] if reverse else gc[C - 1:C, :]
    ke = k * jnp.exp(g_last - gc)
    upd = _dot_tn(ke.astype(BF16), v_new.astype(BF16))
    state_ref[...] = state * jnp.exp(g_last) + jnp.where(same_head, upd, 0.0)
    o_ref[sl, :] = o


def _gdn_kernel(qf, kf, vf, smf, gf, qb, kb, vb, smb, gb, e_ref, tril_ref, triu_ref, cum_f_ref, cum_b_ref,
                of_ref, ob_ref, sf_ref, sb_ref, grow_f_ref, grow_b_ref):
    @pl.when(pl.program_id(1) == 0)
    def _():
        sf_ref[...] = jnp.zeros_like(sf_ref)
        sb_ref[...] = jnp.zeros_like(sb_ref)

    grow_f_ref[...] = _dot_exact(gf[0], cum_f_ref[...])
    grow_b_ref[...] = _dot_exact(gb[0], cum_b_ref[...])
    n = GDN_BLOCK // GDN_CHUNK
    e_beta_f = e_ref[:, 0 * C_W:1 * C_W]
    e_beta_b = e_ref[:, 1 * C_W:2 * C_W]
    e_g_f = e_ref[:, 2 * C_W:3 * C_W]
    e_g_b = e_ref[:, 3 * C_W:4 * C_W]
    tril = tril_ref[...]
    triu = triu_ref[...]

    def body(c, carry):
        _gdn_chunk(c, qf, kf, vf, smf, grow_f_ref, sf_ref, of_ref, e_beta_f, e_g_f, tril, False)
        _gdn_chunk(n - 1 - c, qb, kb, vb, smb, grow_b_ref, sb_ref, ob_ref, e_beta_b, e_g_b, triu, True)
        return carry

    lax.fori_loop(0, n, body, 0)


def _gdn_constants():
    C = GDN_CHUNK
    e = np.zeros((LANES, 4 * C_W), np.float32)
    for kind in range(4):
        for h in range(C_HEADS):
            e[kind * C_HEADS + h, kind * C_W + h * HEAD_DIM:kind * C_W + (h + 1) * HEAD_DIM] = 1.0
    i = np.arange(C)
    tril = (i[None, :] <= i[:, None]).astype(np.float32)
    triu = (i[None, :] >= i[:, None]).astype(np.float32)
    cum_f = np.kron(np.eye(C_HEADS, dtype=np.float32), triu)
    cum_b = np.kron(np.eye(C_HEADS, dtype=np.float32), tril)
    return tuple(jnp.asarray(a) for a in (e, tril, triu, cum_f, cum_b))


def _gated_delta(q, k, v, sm, batch, seq):
    t = q.shape[0]
    nb = seq // GDN_BLOCK
    nc = GDN_BLOCK // GDN_CHUNK
    def rows(col0):
        g = sm[:, col0:col0 + C_HEADS].reshape(t // GDN_CHUNK, GDN_CHUNK, C_HEADS)
        return jnp.transpose(g, (0, 2, 1)).reshape(t // GDN_BLOCK, nc, C_W)

    g_f, g_b = rows(2 * C_HEADS), rows(3 * C_HEADS)
    e, tril, triu, cum_f, cum_b = _gdn_constants()

    def fwd(w):
        return pl.BlockSpec((GDN_BLOCK, w), lambda b, s: (b * nb + s, 0))

    def bwd(w):
        return pl.BlockSpec((GDN_BLOCK, w), lambda b, s: (b * nb + nb - 1 - s, 0))

    gf_spec = pl.BlockSpec((1, nc, C_W), lambda b, s: (b * nb + s, 0, 0))
    gb_spec = pl.BlockSpec((1, nc, C_W), lambda b, s: (b * nb + nb - 1 - s, 0, 0))

    def const(a):
        return pl.BlockSpec(a.shape, lambda b, s: (0, 0))

    return pl.pallas_call(
        _gdn_kernel, grid=(batch, nb),
        in_specs=[fwd(C_W), fwd(C_W), fwd(C_W), fwd(LANES), gf_spec,
                  bwd(C_W), bwd(C_W), bwd(C_W), bwd(LANES), gb_spec,
                  const(e), const(tril), const(triu), const(cum_f), const(cum_b)],
        out_specs=[fwd(C_W), bwd(C_W)],
        out_shape=[jax.ShapeDtypeStruct((t, C_W), F32)] * 2,
        scratch_shapes=[pltpu.VMEM((C_W, C_W), F32), pltpu.VMEM((C_W, C_W), F32),
                        pltpu.VMEM((nc, C_W), F32), pltpu.VMEM((nc, C_W), F32)],
        compiler_params=_cparams("parallel", "arbitrary"), name="gated_delta_rule",
    )(q, k, v, sm, g_f, q, k, v, sm, g_b, e, tril, triu, cum_f, cum_b)


def _merge_kernel(h_ref, ya_ref, yb_ref, of_ref, ob_ref, z_ref, wg_ref, wa_ref, wb_ref, wc_ref, wo_ref,
                  gn_ref, seg_ref, lg_ref, lb_ref, wrh_ref, wrl_ref, rb_ref, h1_ref, route_ref):
    h = h_ref[...]
    x = h.astype(BF16)
    tm = h.shape[0]
    d = D_MODEL
    o = of_ref[...] + ob_ref[...]
    ms = _seg_sum(o * o, seg_ref[...]) * (1.0 / HEAD_DIM)
    yc = o * lax.rsqrt(ms + 1e-6) * gn_ref[...] * _silu(z_ref[...].astype(F32))
    mix = _sigmoid(_dot(x, wg_ref[:, 0:d])) * _dot(ya_ref[...], wa_ref[...])
    mix = mix + _sigmoid(_dot(x, wg_ref[:, d:2 * d])) * _dot(yb_ref[...], wb_ref[...])
    mix = mix + _sigmoid(_dot(x, wg_ref[:, 2 * d:3 * d])) * _dot(yc.astype(BF16), wc_ref[...])
    y = DN_ALPHA * h + _dot(mix.astype(BF16), wo_ref[...])
    h1 = _layernorm_rows(y, lg_ref[...], lb_ref[...])
    h1_ref[...] = h1

    hi = h1.astype(BF16)
    lo = (h1 - hi.astype(F32)).astype(BF16)
    logits = _dot(hi, wrh_ref[...]) + _dot(lo, wrh_ref[...]) + _dot(hi, wrl_ref[...])
    scores = _sigmoid(logits)
    sel = scores + rb_ref[...]
    lane = lax.broadcasted_iota(jnp.int32, (tm, LANES), 1)
    in_grp = lane < N_GROUPS
    sel_m = [sel if m == 0 else pltpu.roll(sel, LANES - m * N_GROUPS, 1) for m in range(GROUP_SIZE)]
    sc_m = [scores if m == 0 else pltpu.roll(scores, LANES - m * N_GROUPS, 1) for m in range(GROUP_SIZE)]
    hi01, lo01 = jnp.maximum(sel_m[0], sel_m[1]), jnp.minimum(sel_m[0], sel_m[1])
    hi23, lo23 = jnp.maximum(sel_m[2], sel_m[3]), jnp.minimum(sel_m[2], sel_m[3])
    top1 = jnp.maximum(hi01, hi23)
    top2 = jnp.maximum(jnp.minimum(hi01, hi23), jnp.maximum(lo01, lo23))
    grp_score = jnp.where(in_grp, top1 + top2, -jnp.inf)
    best_score = jnp.max(grp_score, axis=-1, keepdims=True)
    best = jnp.min(jnp.where(grp_score == best_score, lane, LANES), axis=-1, keepdims=True)
    pick = lane == best
    val = [jnp.sum(jnp.where(pick, s, 0.0), axis=-1, keepdims=True) for s in sel_m]
    aff = [jnp.sum(jnp.where(pick, s, 0.0), axis=-1, keepdims=True) for s in sc_m]
    e_sel = [jnp.zeros((tm, 1), F32)] * TOP_K
    w_sel = [jnp.zeros((tm, 1), F32)] * TOP_K
    for m in range(GROUP_SIZE):
        rank = jnp.zeros((tm, 1), jnp.int32)
        for j in range(GROUP_SIZE):
            if j != m:
                ahead = (val[j] > val[m]) | ((val[j] == val[m]) & (j < m))
                rank = rank + ahead.astype(jnp.int32)
        for kk in range(TOP_K):
            e_sel[kk] = jnp.where(rank == kk, (best * GROUP_SIZE + m).astype(F32), e_sel[kk])
            w_sel[kk] = jnp.where(rank == kk, aff[m], w_sel[kk])
    tot = w_sel[0] + w_sel[1]
    route_ref[...] = jnp.where(lane == 0, e_sel[0], jnp.where(lane == 1, e_sel[1], jnp.where(
        lane == 2, w_sel[0] / tot, jnp.where(lane == 3, w_sel[1] / tot, 0.0))))


def _merge(h, ya, yb, o_f, o_b, z, wg, wa, wb, wc, wo, gn, seg, lg, lb, wr_hi, wr_lo, rb):
    t, d = h.shape
    tm = TOKEN_TILE

    def rows(w):
        return pl.BlockSpec((tm, w), lambda i: (i, 0))

    def const(a):
        return pl.BlockSpec(a.shape, lambda i: (0, 0))

    consts = (wg, wa, wb, wc, wo, gn, seg, lg, lb, wr_hi, wr_lo, rb)
    return pl.pallas_call(
        _merge_kernel, grid=(t // tm,),
        in_specs=[rows(d), rows(A_Q_W), rows(B_W), rows(C_W), rows(C_W), rows(C_W)] + [const(a) for a in consts],
        out_specs=[rows(d), rows(LANES)],
        out_shape=[jax.ShapeDtypeStruct((t, d), F32), jax.ShapeDtypeStruct((t, LANES), F32)],
        compiler_params=_cparams("parallel"), name="merge_outproj_ln_router",
    )(h, ya, yb, o_f, o_b, z, *consts)


def _expert_kernel(te_ref, nv_ref, x_ref, w1_ref, w3_ref, w2_ref, y_ref):
    @pl.when(pl.program_id(0) < nv_ref[0])
    def _():
        x = x_ref[...]
        hid = _silu(_dot(x, w1_ref[0])) * _dot(x, w3_ref[0])
        y_ref[...] = _dot(hid.astype(BF16), w2_ref[0]).astype(BF16)

    @pl.when(pl.program_id(0) >= nv_ref[0])
    def _():
        y_ref[...] = jnp.zeros_like(y_ref)


def _expert_mlp(xs, tile_expert, n_valid, w1, w3, w2):
    mp, d = xs.shape
    te = EXPERT_TILE
    grid_spec = pltpu.PrefetchScalarGridSpec(
        num_scalar_prefetch=2, grid=(mp // te,),
        in_specs=[pl.BlockSpec((te, d), lambda i, e, n: (i, 0)),
                  pl.BlockSpec((1, d, D_EXPERT), lambda i, e, n: (e[i], 0, 0)),
                  pl.BlockSpec((1, d, D_EXPERT), lambda i, e, n: (e[i], 0, 0)),
                  pl.BlockSpec((1, D_EXPERT, d), lambda i, e, n: (e[i], 0, 0))],
        out_specs=pl.BlockSpec((te, d), lambda i, e, n: (i, 0)))
    return pl.pallas_call(
        _expert_kernel, grid_spec=grid_spec, out_shape=jax.ShapeDtypeStruct((mp, d), BF16),
        compiler_params=_cparams("arbitrary"), name="expert_mlp",
    )(tile_expert, n_valid, xs, w1, w3, w2)


def _dispatch_plan(route, t):
    te = EXPERT_TILE
    e = route[:, 0:TOP_K].astype(jnp.int32)
    onehot = (e[:, :, None] == jnp.arange(N_EXPERTS, dtype=jnp.int32)[None, None, :]).astype(jnp.int32).sum(axis=1)
    csum = jnp.cumsum(onehot, axis=0)
    counts = csum[-1]
    padded = ((counts + te - 1) // te) * te
    ends = jnp.cumsum(padded)
    offsets = ends - padded
    rank = jnp.take_along_axis(csum, e, axis=1) - 1
    pos = offsets[e] + rank
    n_tiles = (t * TOP_K) // te + N_EXPERTS
    tile_start = jnp.arange(n_tiles, dtype=jnp.int32) * te
    tile_expert = jnp.minimum(jnp.searchsorted(ends, tile_start, side="right"), N_EXPERTS - 1).astype(jnp.int32)
    n_valid = (ends[-1] // te).astype(jnp.int32).reshape(1)
    src = jnp.zeros((n_tiles * te,), jnp.int32).at[pos.reshape(-1)].set(
        jnp.repeat(jnp.arange(t, dtype=jnp.int32), TOP_K))
    return pos, src, tile_expert, n_valid


def _combine_kernel(h_ref, y0_ref, y1_ref, route_ref, g_ref, b_ref, o_ref):
    lane = lax.broadcasted_iota(jnp.int32, route_ref.shape, 1)
    r = route_ref[...]
    w0 = jnp.sum(jnp.where(lane == 2, r, 0.0), axis=-1, keepdims=True)
    w1 = jnp.sum(jnp.where(lane == 3, r, 0.0), axis=-1, keepdims=True)
    y = w0 * y0_ref[...].astype(F32) + w1 * y1_ref[...].astype(F32)
    o_ref[...] = _layernorm_rows(DN_ALPHA * h_ref[...] + y, g_ref[...], b_ref[...])


def _combine(h1, y0, y1, route, g, b):
    t, d = h1.shape
    row = pl.BlockSpec((TOKEN_TILE, d), lambda i: (i, 0))
    vec = pl.BlockSpec((1, d), lambda i: (0, 0))
    return pl.pallas_call(
        _combine_kernel, grid=(t // TOKEN_TILE,),
        in_specs=[row, row, row, pl.BlockSpec((TOKEN_TILE, LANES), lambda i: (i, 0)), vec, vec],
        out_specs=row, out_shape=jax.ShapeDtypeStruct((t, d), F32),
        compiler_params=_cparams("parallel"), name="combine_ln",
    )(h1, y0, y1, route, g.reshape(1, d), b.reshape(1, d))


def _rope_tables(seq):
    tkn = np.arange(seq)
    pos = np.stack([tkn // GRID_W, tkn % GRID_W], axis=1).astype(np.float32)
    half = HEAD_DIM // 2
    inv = ROPE_THETA ** (-np.arange(0, half, 2, dtype=np.float32) / half)
    d = np.arange(LANES) % HEAD_DIM
    axis = d // half
    r = d % half
    ang = pos[:, axis] * inv[r % (half // 2)][None, :]
    sign = np.where(r < half // 2, -1.0, 1.0)[None, :]
    return jnp.asarray(np.cos(ang), F32), jnp.asarray(np.sin(ang) * sign, F32)


def _segment_matrix(width):
    i = np.arange(width) // HEAD_DIM
    return jnp.asarray((i[:, None] == i[None, :]).astype(np.float32), BF16)


def _router_layout(w_router, router_bias):
    perm = np.array([(l % N_GROUPS) * GROUP_SIZE + l // N_GROUPS for l in range(N_EXPERTS)])
    w = jnp.pad(w_router[:, perm], ((0, 0), (0, LANES - N_EXPERTS)))
    hi = w.astype(BF16)
    lo = (w - hi.astype(F32)).astype(BF16)
    rb = jnp.pad(router_bias[perm], (0, LANES - N_EXPERTS)).reshape(1, LANES)
    return hi, lo, rb


def kernel(x, ln0_g, ln0_b, w_in, q_norm_g, k_norm_g, na_rpb, conv_w, A_log, dt_bias, gdn_norm_g,
           w_branch_a, w_branch_b, w_branch_c, w_out, ln1_g, ln1_b, w_router, router_bias, w1, w3, w2,
           ln2_g, ln2_b):
    batch, seq, d = x.shape
    t = batch * seq
    depth = w_in.shape[0]
    rows = seq // GRID_W
    cos, sin = _rope_tables(seq)
    seg128 = _segment_matrix(LANES)
    seg256 = _segment_matrix(C_W)
    wr_hi, wr_lo, rb = _router_layout(w_router, router_bias)

    h = _input_layernorm(x.reshape(t, d), ln0_g, ln0_b)
    for l in range(depth):
        w_main = w_in[l, :, :MAIN_W].astype(BF16)
        w_small = jnp.pad(w_in[l, :, MAIN_W:MAIN_W + SMALL_W], ((0, 0), (0, LANES - SMALL_W))).astype(BF16)
        w_gate = w_in[l, :, MAIN_W + SMALL_W:].astype(BF16)
        gp = jnp.zeros((8, LANES), F32)
        gp = gp.at[0, 2 * C_HEADS:4 * C_HEADS].set(A_log[l].reshape(-1))
        gp = gp.at[1, 2 * C_HEADS:4 * C_HEADS].set(dt_bias[l].reshape(-1))
        qg = jnp.tile(q_norm_g[l], LANES // HEAD_DIM).reshape(1, LANES)
        kg = jnp.tile(k_norm_g[l], LANES // HEAD_DIM).reshape(1, LANES)
        qpad, ak, av, bq, bk, bv, cqkv, cz, sm = _projection(h, w_main, w_small, cos, sin, qg, kg, seg128, gp, seq)

        ya = _gqa(qpad, ak, av, batch, seq)
        yb = _neighbourhood(bq, bk, bv, _na_bias_table(na_rpb[l], rows), batch, seq)
        conv_pad = jnp.pad(conv_w[l], ((0, 8 - CONV_K), (0, 0)))
        cq, ck, cv = _short_conv_norm(cqkv, conv_pad, seg256, seq)
        o_f, o_b = _gated_delta(cq, ck, cv, sm, batch, seq)

        gn = jnp.tile(gdn_norm_g[l], C_HEADS).reshape(1, C_W)
        h1, route = _merge(h, ya, yb, o_f, o_b, cz, w_gate, w_branch_a[l].astype(BF16),
                           w_branch_b[l].astype(BF16), w_branch_c[l].astype(BF16), w_out[l].astype(BF16),
                           gn, seg256, ln1_g[l].reshape(1, d), ln1_b[l].reshape(1, d), wr_hi, wr_lo, rb)

        pos, src, tile_expert, n_valid = _dispatch_plan(route, t)
        xs = jnp.take(h1.astype(BF16), src, axis=0)
        ys = _expert_mlp(xs, tile_expert, n_valid, w1[l].astype(BF16), w3[l].astype(BF16), w2[l].astype(BF16))
        y0 = jnp.take(ys, pos[:, 0], axis=0)
        y1 = jnp.take(ys, pos[:, 1], axis=0)
        h = _combine(h1, y0, y1, route, ln2_g[l], ln2_b[l])
    return h.reshape(batch, seq, d)
```

```python
import functools
import math

import numpy as np
import jax
import jax.numpy as jnp
from jax import lax
from jax.experimental import pallas as pl
from jax.experimental.pallas import tpu as pltpu

F32 = jnp.float32
BF16 = jnp.bfloat16

D_MODEL = 1024
GRID_W = 64
HEAD_DIM = 64
A_Q_HEADS = 8
A_KV_HEADS = 2
ROPE_THETA = 10000.0
B_HEADS = 4
WIN_R = 8
WIN_C = 16
C_HEADS = 4
CONV_K = 5
N_EXPERTS = 32
N_GROUPS = 8
GROUP_SIZE = N_EXPERTS // N_GROUPS
TOP_K = 2
D_EXPERT = 512
DEPTH = 4
DN_ALPHA = (2 * DEPTH) ** 0.25

A_Q_W = A_Q_HEADS * HEAD_DIM
A_KV_W = A_KV_HEADS * HEAD_DIM
B_W = B_HEADS * HEAD_DIM
C_W = C_HEADS * HEAD_DIM
MAIN_W = A_Q_W + 2 * A_KV_W + 3 * B_W + 4 * C_W
SMALL_W = 4 * C_HEADS

LANES = 128
TOKEN_TILE = 512
ATTN_Q_TILE = 256
NA_ROWS_PER_BLOCK = 8
GDN_CHUNK = 64
GDN_BLOCK = 512
EXPERT_TILE = 256
VMEM_LIMIT = 48 * 1024 * 1024
NEG_BIG = -1e30
HIGHEST = lax.Precision.HIGHEST


def _cparams(*sem):
    return pltpu.CompilerParams(dimension_semantics=sem, vmem_limit_bytes=VMEM_LIMIT)


def _dot(a, b):
    return jnp.dot(a, b, preferred_element_type=F32)


def _dot_nt(a, b):
    return lax.dot_general(a, b, (((1,), (1,)), ((), ())), preferred_element_type=F32)


def _dot_tn(a, b):
    return lax.dot_general(a, b, (((0,), (0,)), ((), ())), preferred_element_type=F32)


def _dot_exact(a, b):
    return jnp.dot(a, b, preferred_element_type=F32, precision=HIGHEST)


def _seg_sum(y, seg):
    hi = y.astype(BF16)
    lo = (y - hi.astype(F32)).astype(BF16)
    return _dot(hi, seg) + _dot(lo, seg)


def _sigmoid(x):
    return 1.0 / (1.0 + jnp.exp(-x))


def _silu(x):
    return x * _sigmoid(x)


def _layernorm_rows(y, g, b):
    mu = jnp.mean(y, axis=-1, keepdims=True)
    yc = y - mu
    var = jnp.mean(yc * yc, axis=-1, keepdims=True)
    return yc * lax.rsqrt(var + 1e-5) * g + b


def _ln_kernel(x_ref, g_ref, b_ref, o_ref):
    o_ref[...] = _layernorm_rows(x_ref[...], g_ref[...], b_ref[...])


def _input_layernorm(x, g, b):
    t, d = x.shape
    row = pl.BlockSpec((TOKEN_TILE, d), lambda i: (i, 0))
    vec = pl.BlockSpec((1, d), lambda i: (0, 0))
    return pl.pallas_call(
        _ln_kernel, grid=(t // TOKEN_TILE,), in_specs=[row, vec, vec], out_specs=row,
        out_shape=jax.ShapeDtypeStruct((t, d), F32), compiler_params=_cparams("parallel"),
        name="input_layernorm")(x, g.reshape(1, d), b.reshape(1, d))


def _proj_kernel(h_ref, w_ref, ws_ref, cos_ref, sin_ref, qg_ref, kg_ref, seg_ref, gp_ref,
                 q_ref, k_ref, v_ref, bq_ref, bk_ref, bv_ref, c_ref, z_ref, sm_ref):
    x = h_ref[...].astype(BF16)
    tm = x.shape[0]
    lane = lax.broadcasted_iota(jnp.int32, (tm, LANES), 1)
    first_half = (lane % (HEAD_DIM // 2)) < (HEAD_DIM // 4)
    low_head = lane < HEAD_DIM
    cos = cos_ref[...]
    sin = sin_ref[...]
    seg = seg_ref[...]

    def mm(a, b):
        return _dot(x, w_ref[:, a:b])

    def norm_rope(y, g):
        ms = _seg_sum(y * y, seg) * (1.0 / HEAD_DIM)
        yn = y * lax.rsqrt(ms + 1e-6) * g
        partner = jnp.where(first_half, pltpu.roll(yn, LANES - HEAD_DIM // 4, 1),
                            pltpu.roll(yn, HEAD_DIM // 4, 1))
        return yn * cos + partner * sin

    for j in range(A_Q_W // LANES):
        y = norm_rope(mm(j * LANES, (j + 1) * LANES), qg_ref[...]) * (HEAD_DIM ** -0.5)
        yr = pltpu.roll(y, HEAD_DIM, 1)
        kv_head = (2 * j) // (A_Q_HEADS // A_KV_HEADS)
        if kv_head == 0:
            even, odd = jnp.where(low_head, y, 0.0), jnp.where(low_head, yr, 0.0)
        else:
            even, odd = jnp.where(low_head, 0.0, yr), jnp.where(low_head, 0.0, y)
        q_ref[:, (2 * j) * LANES:(2 * j + 1) * LANES] = even.astype(BF16)
        q_ref[:, (2 * j + 1) * LANES:(2 * j + 2) * LANES] = odd.astype(BF16)
    o = A_Q_W
    k_ref[...] = norm_rope(mm(o, o + A_KV_W), kg_ref[...]).astype(BF16)
    o += A_KV_W
    v_ref[...] = mm(o, o + A_KV_W).astype(BF16)
    o += A_KV_W
    bq_ref[...] = (mm(o, o + B_W) * (HEAD_DIM ** -0.5)).astype(BF16)
    o += B_W
    bk_ref[...] = mm(o, o + B_W).astype(BF16)
    o += B_W
    bv_ref[...] = mm(o, o + B_W).astype(BF16)
    o += B_W
    c_ref[...] = mm(o, o + 3 * C_W)
    o += 3 * C_W
    z_ref[...] = mm(o, o + C_W).astype(BF16)

    s = _dot(x, ws_ref[...])
    a = jnp.exp(gp_ref[0:1, :])
    u = s + gp_ref[1:2, :]
    softplus = jnp.maximum(u, 0.0) + jnp.log(1.0 + jnp.exp(-jnp.abs(u)))
    sm_ref[...] = jnp.where(lane < 2 * C_HEADS, _sigmoid(s), jnp.where(lane < SMALL_W, -a * softplus, 0.0))


def _projection(h, w_main, w_small, cos, sin, qg, kg, seg, gp, seq):
    t, d = h.shape
    tm = TOKEN_TILE
    nseq = seq // tm

    def rows(w):
        return pl.BlockSpec((tm, w), lambda i: (i, 0))

    def const(shape):
        return pl.BlockSpec(shape, lambda i: (0, 0))

    pos = pl.BlockSpec((tm, LANES), lambda i: (i % nseq, 0))
    out_w = [(2 * A_Q_W, BF16), (A_KV_W, BF16), (A_KV_W, BF16), (B_W, BF16), (B_W, BF16), (B_W, BF16),
             (3 * C_W, F32), (C_W, BF16), (LANES, F32)]
    return pl.pallas_call(
        _proj_kernel, grid=(t // tm,),
        in_specs=[rows(d), const((d, MAIN_W)), const((d, LANES)), pos, pos, const((1, LANES)),
                  const((1, LANES)), const((LANES, LANES)), const((8, LANES))],
        out_specs=[rows(w) for w, _ in out_w],
        out_shape=[jax.ShapeDtypeStruct((t, w), dt) for w, dt in out_w],
        compiler_params=_cparams("parallel"), name="projection",
    )(h, w_main, w_small, cos, sin, qg, kg, seg, gp)


def _gqa_kernel(q_ref, k_ref, v_ref, o_ref):
    tq = q_ref.shape[0]
    lane = lax.broadcasted_iota(jnp.int32, (tq, LANES), 1)
    low_head = lane < HEAD_DIM
    k = k_ref[...]
    v = v_ref[...]
    group = A_Q_HEADS // A_KV_HEADS
    outs = []
    for h in range(A_Q_HEADS):
        s = _dot_nt(q_ref[:, h * LANES:(h + 1) * LANES], k)
        m = jnp.max(s, axis=-1, keepdims=True)
        p = jnp.exp(s - m)
        l = jnp.sum(p, axis=-1, keepdims=True)
        outs.append(_dot(p.astype(BF16), v) / l)
    for j in range(A_Q_HEADS // 2):
        even, odd = outs[2 * j], outs[2 * j + 1]
        if (2 * j) // group == 0:
            blk = jnp.where(low_head, even, pltpu.roll(odd, HEAD_DIM, 1))
        else:
            blk = jnp.where(low_head, pltpu.roll(even, HEAD_DIM, 1), odd)
        o_ref[:, j * LANES:(j + 1) * LANES] = blk.astype(BF16)


def _gqa(qpad, k, v, batch, seq):
    t = qpad.shape[0]
    tq = ATTN_Q_TILE
    nq = seq // tq
    kv = pl.BlockSpec((seq, A_KV_W), lambda b, i: (b, 0))
    return pl.pallas_call(
        _gqa_kernel, grid=(batch, nq),
        in_specs=[pl.BlockSpec((tq, 2 * A_Q_W), lambda b, i: (b * nq + i, 0)), kv, kv],
        out_specs=pl.BlockSpec((tq, A_Q_W), lambda b, i: (b * nq + i, 0)),
        out_shape=jax.ShapeDtypeStruct((t, A_Q_W), BF16),
        compiler_params=_cparams("parallel", "parallel"), name="gqa_attention",
    )(qpad, k, v)


def _na_key_row0(qb, rows):
    return min(max(NA_ROWS_PER_BLOCK * qb - WIN_R // 2, 0), rows - 2 * NA_ROWS_PER_BLOCK)


def _na_kernel(q_ref, k_ref, v_ref, bias_ref, o_ref, *, rows):
    qb = pl.program_id(0)
    nq = q_ref.shape[0]
    nk = 2 * NA_ROWS_PER_BLOCK * GRID_W
    row0 = jnp.clip(NA_ROWS_PER_BLOCK * qb - WIN_R // 2, 0, rows - 2 * NA_ROWS_PER_BLOCK)
    start = pl.multiple_of(row0 * GRID_W, 4 * GRID_W)
    lane = lax.broadcasted_iota(jnp.int32, (nq, LANES), 1)
    low_head = lane < HEAD_DIM
    zero = jnp.zeros((), BF16)
    for j in range(B_W // LANES):
        kblk = k_ref[pl.ds(start, nk), j * LANES:(j + 1) * LANES]
        vblk = v_ref[pl.ds(start, nk), j * LANES:(j + 1) * LANES]
        qblk = q_ref[:, j * LANES:(j + 1) * LANES]
        outs = []
        for par in range(2):
            qh = jnp.where(low_head if par == 0 else jnp.logical_not(low_head), qblk, zero)
            s = _dot_nt(qh, kblk) + bias_ref[0, 2 * j + par]
            m = jnp.max(s, axis=-1, keepdims=True)
            p = jnp.exp(s - m)
            l = jnp.sum(p, axis=-1, keepdims=True)
            outs.append(_dot(p.astype(BF16), vblk) / l)
        o_ref[:, j * LANES:(j + 1) * LANES] = jnp.where(low_head, outs[0], outs[1]).astype(BF16)


def _na_bias_table(rpb, rows):
    nqb = rows // NA_ROWS_PER_BLOCK
    nkr = 2 * NA_ROWS_PER_BLOCK
    wr = min(WIN_R, rows)
    n_dr = 2 * WIN_R - 1
    c = np.arange(GRID_W)
    c0 = np.clip(c - WIN_C // 2, 0, GRID_W - WIN_C)
    in_c = (c[None, :] >= c0[:, None]) & (c[None, :] < c0[:, None] + WIN_C)
    dc = np.clip(c[None, :] - c[:, None] + (WIN_C - 1), 0, 2 * WIN_C - 2)
    col_tiles = jnp.where(jnp.asarray(in_c)[None, None], rpb[:, :, jnp.asarray(dc)], NEG_BIG)
    col_tiles = jnp.concatenate([col_tiles, jnp.full((B_HEADS, 1, GRID_W, GRID_W), NEG_BIG, F32)], axis=1)
    tile_idx = np.full((nqb, NA_ROWS_PER_BLOCK, nkr), n_dr, np.int32)
    for qb in range(nqb):
        for rq in range(NA_ROWS_PER_BLOCK):
            r = NA_ROWS_PER_BLOCK * qb + rq
            r0 = min(max(r - wr // 2, 0), rows - wr)
            for rk in range(nkr):
                kr = _na_key_row0(qb, rows) + rk
                if r0 <= kr < r0 + wr:
                    tile_idx[qb, rq, rk] = kr - r + (WIN_R - 1)
    table = jnp.take(col_tiles, jnp.asarray(tile_idx.reshape(-1)), axis=1)
    table = table.reshape(B_HEADS, nqb, NA_ROWS_PER_BLOCK, nkr, GRID_W, GRID_W)
    table = jnp.transpose(table, (1, 0, 2, 4, 3, 5))
    return table.reshape(nqb, B_HEADS, NA_ROWS_PER_BLOCK * GRID_W, nkr * GRID_W)


def _neighbourhood(q, k, v, bias, batch, seq):
    t = q.shape[0]
    rows = seq // GRID_W
    nqb = rows // NA_ROWS_PER_BLOCK
    nq = NA_ROWS_PER_BLOCK * GRID_W
    kv = pl.BlockSpec((seq, B_W), lambda qb, b: (b, 0))
    qo = pl.BlockSpec((nq, B_W), lambda qb, b: (b * nqb + qb, 0))
    return pl.pallas_call(
        functools.partial(_na_kernel, rows=rows), grid=(nqb, batch),
        in_specs=[qo, kv, kv, pl.BlockSpec((1, B_HEADS, nq, 2 * nq), lambda qb, b: (qb, 0, 0, 0))],
        out_specs=qo, out_shape=jax.ShapeDtypeStruct((t, B_W), BF16),
        compiler_params=_cparams("parallel", "parallel"), name="neighbourhood_attention",
    )(q, k, v, bias)


def _conv_kernel(c_ref, prev_ref, next_ref, w_ref, seg_ref, q_ref, k_ref, v_ref, pad_ref, *, tiles_per_seq):
    i = pl.program_id(0)
    tm = c_ref.shape[0]
    halo = prev_ref.shape[0]
    first = (i % tiles_per_seq) == 0
    last = (i % tiles_per_seq) == tiles_per_seq - 1
    pad_ref[0:halo, :] = jnp.where(first, 0.0, prev_ref[...])
    pad_ref[halo:halo + tm, :] = c_ref[...]
    pad_ref[halo + tm:2 * halo + tm, :] = jnp.where(last, 0.0, next_ref[...])
    acc = None
    for j in range(CONV_K):
        o = halo - CONV_K // 2 + j
        term = w_ref[j:j + 1, :] * pad_ref[o:o + tm, :]
        acc = term if acc is None else acc + term
    y = _silu(acc)
    seg = seg_ref[...]

    def l2n(u):
        return u * lax.rsqrt(_seg_sum(u * u, seg) + 1e-6)

    q_ref[...] = (l2n(y[:, 0:C_W]) * (HEAD_DIM ** -0.5)).astype(BF16)
    k_ref[...] = l2n(y[:, C_W:2 * C_W]).astype(BF16)
    v_ref[...] = y[:, 2 * C_W:3 * C_W].astype(BF16)


def _short_conv_norm(c, conv_w, seg, seq):
    t, w = c.shape
    tm = TOKEN_TILE
    halo = 8
    nb = tm // halo
    last_blk = t // halo - 1
    out = pl.BlockSpec((tm, C_W), lambda i: (i, 0))
    return pl.pallas_call(
        functools.partial(_conv_kernel, tiles_per_seq=seq // tm), grid=(t // tm,),
        in_specs=[pl.BlockSpec((tm, w), lambda i: (i, 0)),
                  pl.BlockSpec((halo, w), lambda i: (jnp.maximum(i * nb - 1, 0), 0)),
                  pl.BlockSpec((halo, w), lambda i: (jnp.minimum((i + 1) * nb, last_blk), 0)),
                  pl.BlockSpec((8, w), lambda i: (0, 0)),
                  pl.BlockSpec((C_W, C_W), lambda i: (0, 0))],
        out_specs=[out, out, out],
        out_shape=[jax.ShapeDtypeStruct((t, C_W), BF16)] * 3,
        scratch_shapes=[pltpu.VMEM((tm + 2 * halo, w), F32)],
        compiler_params=_cparams("parallel"), name="gdn_conv_norm",
    )(c, c, c, conv_w, seg)


def _block_diag(x, same_head):
    return jnp.where(same_head, jnp.concatenate([x] * C_HEADS, axis=0), 0.0)


def _gdn_chunk(c, q_ref, k_ref, v_ref, sm_ref, grow_ref, state_ref, o_ref, e_beta, e_g, tri, reverse):
    C = GDN_CHUNK
    sl = pl.ds(pl.multiple_of(c * C, C), C)
    q = q_ref[sl, :].astype(F32)
    k = k_ref[sl, :].astype(F32)
    v = v_ref[sl, :].astype(F32)
    sm = sm_ref[sl, :]
    beta = _dot_exact(sm, e_beta)
    gc = _dot_exact(tri, _dot_exact(sm, e_g))
    grow = grow_ref[pl.ds(c, 1), :]

    ri = lax.broadcasted_iota(jnp.int32, (C, C_W), 0)
    lj = lax.broadcasted_iota(jnp.int32, (C, C_W), 1) % C
    incl = (ri <= lj) if reverse else (ri >= lj)
    strict = (ri < lj) if reverse else (ri > lj)
    rr = lax.broadcasted_iota(jnp.int32, (C_HEADS * C, C_W), 0) // C
    ll = lax.broadcasted_iota(jnp.int32, (C_HEADS * C, C_W), 1) // HEAD_DIM
    same_head = rr == ll

    def bd(x):
        return _block_diag(x, same_head).astype(BF16)

    decay = jnp.exp(jnp.where(incl, gc - grow, -jnp.inf))
    kb = k * beta
    r = _dot_nt(jnp.concatenate([kb, q], axis=0).astype(BF16), bd(k))
    m = jnp.where(strict, r[:C] * decay, 0.0)
    qk = r[C:] * decay

    p = jnp.where(ri == lj, 1.0, 0.0) - m
    a = _dot(m.astype(BF16), bd(m))
    for _ in range(int(math.log2(C)) - 2):
        r2 = _dot(jnp.concatenate([a, p], axis=0).astype(BF16), bd(a))
        a, p = r2[:C], p + r2[C:]
    t_inv = p + _dot(p.astype(BF16), bd(a))

    egc = jnp.exp(gc)
    rhs = jnp.concatenate([bd(v * beta), bd(kb * egc)], axis=1)
    uw = _dot(t_inv.astype(BF16), rhs)
    u, w = uw[:, :C_W], uw[:, C_W:]

    state = state_ref[...]
    state_b = state.astype(BF16)
    v_new = u - _dot(w.astype(BF16), state_b)
    o = _dot((q * egc).astype(BF16), state_b) + _dot(qk.astype(BF16), bd(v_new))
    g_last = gc[0:1, :] if reverse else gc[C - 1:C, :]
    ke = k * jnp.exp(g_last - gc)
    upd = _dot_tn(ke.astype(BF16), v_new.astype(BF16))
    state_ref[...] = state * jnp.exp(g_last) + jnp.where(same_head, upd, 0.0)
    o_ref[sl, :] = o


def _gdn_kernel(qf, kf, vf, smf, gf, qb, kb, vb, smb, gb, e_ref, tril_ref, triu_ref, cum_f_ref, cum_b_ref,
                of_ref, ob_ref, sf_ref, sb_ref, grow_f_ref, grow_b_ref):
    @pl.when(pl.program_id(1) == 0)
    def _():
        sf_ref[...] = jnp.zeros_like(sf_ref)
        sb_ref[...] = jnp.zeros_like(sb_ref)

    grow_f_ref[...] = _dot_exact(gf[0], cum_f_ref[...])
    grow_b_ref[...] = _dot_exact(gb[0], cum_b_ref[...])
    n = GDN_BLOCK // GDN_CHUNK
    e_beta_f = e_ref[:, 0 * C_W:1 * C_W]
    e_beta_b = e_ref[:, 1 * C_W:2 * C_W]
    e_g_f = e_ref[:, 2 * C_W:3 * C_W]
    e_g_b = e_ref[:, 3 * C_W:4 * C_W]
    tril = tril_ref[...]
    triu = triu_ref[...]

    def body(c, carry):
        _gdn_chunk(c, qf, kf, vf, smf, grow_f_ref, sf_ref, of_ref, e_beta_f, e_g_f, tril, False)
        _gdn_chunk(n - 1 - c, qb, kb, vb, smb, grow_b_ref, sb_ref, ob_ref, e_beta_b, e_g_b, triu, True)
        return carry

    lax.fori_loop(0, n, body, 0)


def _gdn_constants():
    C = GDN_CHUNK
    e = np.zeros((LANES, 4 * C_W), np.float32)
    for kind in range(4):
        for h in range(C_HEADS):
            e[kind * C_HEADS + h, kind * C_W + h * HEAD_DIM:kind * C_W + (h + 1) * HEAD_DIM] = 1.0
    i = np.arange(C)
    tril = (i[None, :] <= i[:, None]).astype(np.float32)
    triu = (i[None, :] >= i[:, None]).astype(np.float32)
    cum_f = np.kron(np.eye(C_HEADS, dtype=np.float32), triu)
    cum_b = np.kron(np.eye(C_HEADS, dtype=np.float32), tril)
    return tuple(jnp.asarray(a) for a in (e, tril, triu, cum_f, cum_b))


def _gated_delta(q, k, v, sm, batch, seq):
    t = q.shape[0]
    nb = seq // GDN_BLOCK
    nc = GDN_BLOCK // GDN_CHUNK
    def rows(col0):
        g = sm[:, col0:col0 + C_HEADS].reshape(t // GDN_CHUNK, GDN_CHUNK, C_HEADS)
        return jnp.transpose(g, (0, 2, 1)).reshape(t // GDN_BLOCK, nc, C_W)

    g_f, g_b = rows(2 * C_HEADS), rows(3 * C_HEADS)
    e, tril, triu, cum_f, cum_b = _gdn_constants()

    def fwd(w):
        return pl.BlockSpec((GDN_BLOCK, w), lambda b, s: (b * nb + s, 0))

    def bwd(w):
        return pl.BlockSpec((GDN_BLOCK, w), lambda b, s: (b * nb + nb - 1 - s, 0))

    gf_spec = pl.BlockSpec((1, nc, C_W), lambda b, s: (b * nb + s, 0, 0))
    gb_spec = pl.BlockSpec((1, nc, C_W), lambda b, s: (b * nb + nb - 1 - s, 0, 0))

    def const(a):
        return pl.BlockSpec(a.shape, lambda b, s: (0, 0))

    return pl.pallas_call(
        _gdn_kernel, grid=(batch, nb),
        in_specs=[fwd(C_W), fwd(C_W), fwd(C_W), fwd(LANES), gf_spec,
                  bwd(C_W), bwd(C_W), bwd(C_W), bwd(LANES), gb_spec,
                  const(e), const(tril), const(triu), const(cum_f), const(cum_b)],
        out_specs=[fwd(C_W), bwd(C_W)],
        out_shape=[jax.ShapeDtypeStruct((t, C_W), F32)] * 2,
        scratch_shapes=[pltpu.VMEM((C_W, C_W), F32), pltpu.VMEM((C_W, C_W), F32),
                        pltpu.VMEM((nc, C_W), F32), pltpu.VMEM((nc, C_W), F32)],
        compiler_params=_cparams("parallel", "arbitrary"), name="gated_delta_rule",
    )(q, k, v, sm, g_f, q, k, v, sm, g_b, e, tril, triu, cum_f, cum_b)


def _merge_kernel(h_ref, ya_ref, yb_ref, of_ref, ob_ref, z_ref, wg_ref, wa_ref, wb_ref, wc_ref, wo_ref,
                  gn_ref, seg_ref, lg_ref, lb_ref, wrh_ref, wrl_ref, rb_ref, h1_ref, route_ref):
    h = h_ref[...]
    x = h.astype(BF16)
    tm = h.shape[0]
    d = D_MODEL
    o = of_ref[...] + ob_ref[...]
    ms = _seg_sum(o * o, seg_ref[...]) * (1.0 / HEAD_DIM)
    yc = o * lax.rsqrt(ms + 1e-6) * gn_ref[...] * _silu(z_ref[...].astype(F32))
    mix = _sigmoid(_dot(x, wg_ref[:, 0:d])) * _dot(ya_ref[...], wa_ref[...])
    mix = mix + _sigmoid(_dot(x, wg_ref[:, d:2 * d])) * _dot(yb_ref[...], wb_ref[...])
    mix = mix + _sigmoid(_dot(x, wg_ref[:, 2 * d:3 * d])) * _dot(yc.astype(BF16), wc_ref[...])
    y = DN_ALPHA * h + _dot(mix.astype(BF16), wo_ref[...])
    h1 = _layernorm_rows(y, lg_ref[...], lb_ref[...])
    h1_ref[...] = h1

    hi = h1.astype(BF16)
    lo = (h1 - hi.astype(F32)).astype(BF16)
    logits = _dot(hi, wrh_ref[...]) + _dot(lo, wrh_ref[...]) + _dot(hi, wrl_ref[...])
    scores = _sigmoid(logits)
    sel = scores + rb_ref[...]
    lane = lax.broadcasted_iota(jnp.int32, (tm, LANES), 1)
    in_grp = lane < N_GROUPS
    sel_m = [sel if m == 0 else pltpu.roll(sel, LANES - m * N_GROUPS, 1) for m in range(GROUP_SIZE)]
    sc_m = [scores if m == 0 else pltpu.roll(scores, LANES - m * N_GROUPS, 1) for m in range(GROUP_SIZE)]
    hi01, lo01 = jnp.maximum(sel_m[0], sel_m[1]), jnp.minimum(sel_m[0], sel_m[1])
    hi23, lo23 = jnp.maximum(sel_m[2], sel_m[3]), jnp.minimum(sel_m[2], sel_m[3])
    top1 = jnp.maximum(hi01, hi23)
    top2 = jnp.maximum(jnp.minimum(hi01, hi23), jnp.maximum(lo01, lo23))
    grp_score = jnp.where(in_grp, top1 + top2, -jnp.inf)
    best_score = jnp.max(grp_score, axis=-1, keepdims=True)
    best = jnp.min(jnp.where(grp_score == best_score, lane, LANES), axis=-1, keepdims=True)
    pick = lane == best
    val = [jnp.sum(jnp.where(pick, s, 0.0), axis=-1, keepdims=True) for s in sel_m]
    aff = [jnp.sum(jnp.where(pick, s, 0.0), axis=-1, keepdims=True) for s in sc_m]
    e_sel = [jnp.zeros((tm, 1), F32)] * TOP_K
    w_sel = [jnp.zeros((tm, 1), F32)] * TOP_K
    for m in range(GROUP_SIZE):
        rank = jnp.zeros((tm, 1), jnp.int32)
        for j in range(GROUP_SIZE):
            if j != m:
                ahead = (val[j] > val[m]) | ((val[j] == val[m]) & (j < m))
                rank = rank + ahead.astype(jnp.int32)
        for kk in range(TOP_K):
            e_sel[kk] = jnp.where(rank == kk, (best * GROUP_SIZE + m).astype(F32), e_sel[kk])
            w_sel[kk] = jnp.where(rank == kk, aff[m], w_sel[kk])
    tot = w_sel[0] + w_sel[1]
    route_ref[...] = jnp.where(lane == 0, e_sel[0], jnp.where(lane == 1, e_sel[1], jnp.where(
        lane == 2, w_sel[0] / tot, jnp.where(lane == 3, w_sel[1] / tot, 0.0))))


def _merge(h, ya, yb, o_f, o_b, z, wg, wa, wb, wc, wo, gn, seg, lg, lb, wr_hi, wr_lo, rb):
    t, d = h.shape
    tm = TOKEN_TILE

    def rows(w):
        return pl.BlockSpec((tm, w), lambda i: (i, 0))

    def const(a):
        return pl.BlockSpec(a.shape, lambda i: (0, 0))

    consts = (wg, wa, wb, wc, wo, gn, seg, lg, lb, wr_hi, wr_lo, rb)
    return pl.pallas_call(
        _merge_kernel, grid=(t // tm,),
        in_specs=[rows(d), rows(A_Q_W), rows(B_W), rows(C_W), rows(C_W), rows(C_W)] + [const(a) for a in consts],
        out_specs=[rows(d), rows(LANES)],
        out_shape=[jax.ShapeDtypeStruct((t, d), F32), jax.ShapeDtypeStruct((t, LANES), F32)],
        compiler_params=_cparams("parallel"), name="merge_outproj_ln_router",
    )(h, ya, yb, o_f, o_b, z, *consts)


def _expert_kernel(te_ref, nv_ref, x_ref, w1_ref, w3_ref, w2_ref, y_ref):
    @pl.when(pl.program_id(0) < nv_ref[0])
    def _():
        x = x_ref[...]
        hid = _silu(_dot(x, w1_ref[0])) * _dot(x, w3_ref[0])
        y_ref[...] = _dot(hid.astype(BF16), w2_ref[0]).astype(BF16)

    @pl.when(pl.program_id(0) >= nv_ref[0])
    def _():
        y_ref[...] = jnp.zeros_like(y_ref)


def _expert_mlp(xs, tile_expert, n_valid, w1, w3, w2):
    mp, d = xs.shape
    te = EXPERT_TILE
    grid_spec = pltpu.PrefetchScalarGridSpec(
        num_scalar_prefetch=2, grid=(mp // te,),
        in_specs=[pl.BlockSpec((te, d), lambda i, e, n: (i, 0)),
                  pl.BlockSpec((1, d, D_EXPERT), lambda i, e, n: (e[i], 0, 0)),
                  pl.BlockSpec((1, d, D_EXPERT), lambda i, e, n: (e[i], 0, 0)),
                  pl.BlockSpec((1, D_EXPERT, d), lambda i, e, n: (e[i], 0, 0))],
        out_specs=pl.BlockSpec((te, d), lambda i, e, n: (i, 0)))
    return pl.pallas_call(
        _expert_kernel, grid_spec=grid_spec, out_shape=jax.ShapeDtypeStruct((mp, d), BF16),
        compiler_params=_cparams("arbitrary"), name="expert_mlp",
    )(tile_expert, n_valid, xs, w1, w3, w2)


def _dispatch_plan(route, t):
    te = EXPERT_TILE
    e = route[:, 0:TOP_K].astype(jnp.int32)
    onehot = (e[:, :, None] == jnp.arange(N_EXPERTS, dtype=jnp.int32)[None, None, :]).astype(jnp.int32).sum(axis=1)
    csum = jnp.cumsum(onehot, axis=0)
    counts = csum[-1]
    padded = ((counts + te - 1) // te) * te
    ends = jnp.cumsum(padded)
    offsets = ends - padded
    rank = jnp.take_along_axis(csum, e, axis=1) - 1
    pos = offsets[e] + rank
    n_tiles = (t * TOP_K) // te + N_EXPERTS
    tile_start = jnp.arange(n_tiles, dtype=jnp.int32) * te
    tile_expert = jnp.minimum((tile_start[:, None] >= ends[None, :]).astype(jnp.int32).sum(axis=1), N_EXPERTS - 1)
    n_valid = (ends[-1] // te).astype(jnp.int32).reshape(1)
    src = jnp.zeros((n_tiles * te,), jnp.int32).at[pos.reshape(-1)].set(
        jnp.repeat(jnp.arange(t, dtype=jnp.int32), TOP_K))
    return pos, src, tile_expert, n_valid


def _combine_kernel(h_ref, y0_ref, y1_ref, route_ref, g_ref, b_ref, o_ref):
    lane = lax.broadcasted_iota(jnp.int32, route_ref.shape, 1)
    r = route_ref[...]
    w0 = jnp.sum(jnp.where(lane == 2, r, 0.0), axis=-1, keepdims=True)
    w1 = jnp.sum(jnp.where(lane == 3, r, 0.0), axis=-1, keepdims=True)
    y = w0 * y0_ref[...].astype(F32) + w1 * y1_ref[...].astype(F32)
    o_ref[...] = _layernorm_rows(DN_ALPHA * h_ref[...] + y, g_ref[...], b_ref[...])


def _combine(h1, y0, y1, route, g, b):
    t, d = h1.shape
    row = pl.BlockSpec((TOKEN_TILE, d), lambda i: (i, 0))
    vec = pl.BlockSpec((1, d), lambda i: (0, 0))
    return pl.pallas_call(
        _combine_kernel, grid=(t // TOKEN_TILE,),
        in_specs=[row, row, row, pl.BlockSpec((TOKEN_TILE, LANES), lambda i: (i, 0)), vec, vec],
        out_specs=row, out_shape=jax.ShapeDtypeStruct((t, d), F32),
        compiler_params=_cparams("parallel"), name="combine_ln",
    )(h1, y0, y1, route, g.reshape(1, d), b.reshape(1, d))


def _rope_tables(seq):
    tkn = np.arange(seq)
    pos = np.stack([tkn // GRID_W, tkn % GRID_W], axis=1).astype(np.float32)
    half = HEAD_DIM // 2
    inv = ROPE_THETA ** (-np.arange(0, half, 2, dtype=np.float32) / half)
    d = np.arange(LANES) % HEAD_DIM
    axis = d // half
    r = d % half
    ang = pos[:, axis] * inv[r % (half // 2)][None, :]
    sign = np.where(r < half // 2, -1.0, 1.0)[None, :]
    return jnp.asarray(np.cos(ang), F32), jnp.asarray(np.sin(ang) * sign, F32)


def _segment_matrix(width):
    i = np.arange(width) // HEAD_DIM
    return jnp.asarray((i[:, None] == i[None, :]).astype(np.float32), BF16)


def _router_layout(w_router, router_bias):
    perm = np.array([(l % N_GROUPS) * GROUP_SIZE + l // N_GROUPS for l in range(N_EXPERTS)])
    w = jnp.pad(w_router[:, perm], ((0, 0), (0, LANES - N_EXPERTS)))
    hi = w.astype(BF16)
    lo = (w - hi.astype(F32)).astype(BF16)
    rb = jnp.pad(router_bias[perm], (0, LANES - N_EXPERTS)).reshape(1, LANES)
    return hi, lo, rb


def kernel(x, ln0_g, ln0_b, w_in, q_norm_g, k_norm_g, na_rpb, conv_w, A_log, dt_bias, gdn_norm_g,
           w_branch_a, w_branch_b, w_branch_c, w_out, ln1_g, ln1_b, w_router, router_bias, w1, w3, w2,
           ln2_g, ln2_b):
    batch, seq, d = x.shape
    t = batch * seq
    depth = w_in.shape[0]
    rows = seq // GRID_W
    cos, sin = _rope_tables(seq)
    seg128 = _segment_matrix(LANES)
    seg256 = _segment_matrix(C_W)
    wr_hi, wr_lo, rb = _router_layout(w_router, router_bias)

    h = _input_layernorm(x.reshape(t, d), ln0_g, ln0_b)
    for l in range(depth):
        w_main = w_in[l, :, :MAIN_W].astype(BF16)
        w_small = jnp.pad(w_in[l, :, MAIN_W:MAIN_W + SMALL_W], ((0, 0), (0, LANES - SMALL_W))).astype(BF16)
        w_gate = w_in[l, :, MAIN_W + SMALL_W:].astype(BF16)
        gp = jnp.zeros((8, LANES), F32)
        gp = gp.at[0, 2 * C_HEADS:4 * C_HEADS].set(A_log[l].reshape(-1))
        gp = gp.at[1, 2 * C_HEADS:4 * C_HEADS].set(dt_bias[l].reshape(-1))
        qg = jnp.tile(q_norm_g[l], LANES // HEAD_DIM).reshape(1, LANES)
        kg = jnp.tile(k_norm_g[l], LANES // HEAD_DIM).reshape(1, LANES)
        qpad, ak, av, bq, bk, bv, cqkv, cz, sm = _projection(h, w_main, w_small, cos, sin, qg, kg, seg128, gp, seq)

        ya = _gqa(qpad, ak, av, batch, seq)
        yb = _neighbourhood(bq, bk, bv, _na_bias_table(na_rpb[l], rows), batch, seq)
        conv_pad = jnp.pad(conv_w[l], ((0, 8 - CONV_K), (0, 0)))
        cq, ck, cv = _short_conv_norm(cqkv, conv_pad, seg256, seq)
        o_f, o_b = _gated_delta(cq, ck, cv, sm, batch, seq)

        gn = jnp.tile(gdn_norm_g[l], C_HEADS).reshape(1, C_W)
        h1, route = _merge(h, ya, yb, o_f, o_b, cz, w_gate, w_branch_a[l].astype(BF16),
                           w_branch_b[l].astype(BF16), w_branch_c[l].astype(BF16), w_out[l].astype(BF16),
                           gn, seg256, ln1_g[l].reshape(1, d), ln1_b[l].reshape(1, d), wr_hi, wr_lo, rb)

        pos, src, tile_expert, n_valid = _dispatch_plan(route, t)
        xs = jnp.take(h1.astype(BF16), src, axis=0)
        ys = _expert_mlp(xs, tile_expert, n_valid, w1[l].astype(BF16), w3[l].astype(BF16), w2[l].astype(BF16))
        y0 = jnp.take(ys, pos[:, 0], axis=0)
        y1 = jnp.take(ys, pos[:, 1], axis=0)
        h = _combine(h1, y0, y1, route, ln2_g[l], ln2_b[l])
    return h.reshape(batch, seq, d)
```

```python
import functools
import math

import numpy as np
import jax
import jax.numpy as jnp
from jax import lax
from jax.experimental import pallas as pl
from jax.experimental.pallas import tpu as pltpu

F32 = jnp.float32
BF16 = jnp.bfloat16

D_MODEL = 1024
GRID_W = 64
HEAD_DIM = 64
A_Q_HEADS = 8
A_KV_HEADS = 2
ROPE_THETA = 10000.0
B_HEADS = 4
WIN_R = 8
WIN_C = 16
C_HEADS = 4
CONV_K = 5
N_EXPERTS = 32
N_GROUPS = 8
GROUP_SIZE = N_EXPERTS // N_GROUPS
TOP_K = 2
D_EXPERT = 512
DEPTH = 4
DN_ALPHA = (2 * DEPTH) ** 0.25

A_Q_W = A_Q_HEADS * HEAD_DIM
A_KV_W = A_KV_HEADS * HEAD_DIM
B_W = B_HEADS * HEAD_DIM
C_W = C_HEADS * HEAD_DIM
MAIN_W = A_Q_W + 2 * A_KV_W + 3 * B_W + 4 * C_W
SMALL_W = 4 * C_HEADS

LANES = 128
TOKEN_TILE = 512
ATTN_Q_TILE = 512
NA_ROWS_PER_BLOCK = 8
GDN_CHUNK = 64
GDN_BLOCK = 512
EXPERT_TILE = 256
VMEM_LIMIT = 48 * 1024 * 1024
NEG_BIG = -1e30
LOG2_E = math.log2(math.e)
HIGHEST = lax.Precision.HIGHEST


def _cparams(*sem):
    return pltpu.CompilerParams(dimension_semantics=sem, vmem_limit_bytes=VMEM_LIMIT)


def _dot(a, b):
    return jnp.dot(a, b, preferred_element_type=F32)


def _dot_nt(a, b):
    return lax.dot_general(a, b, (((1,), (1,)), ((), ())), preferred_element_type=F32)


def _dot_tn(a, b):
    return lax.dot_general(a, b, (((0,), (0,)), ((), ())), preferred_element_type=F32)


def _dot_exact(a, b):
    return jnp.dot(a, b, preferred_element_type=F32, precision=HIGHEST)


def _seg_sum(y, seg):
    hi = y.astype(BF16)
    lo = (y - hi.astype(F32)).astype(BF16)
    return _dot(hi, seg) + _dot(lo, seg)


def _sigmoid(x):
    return 1.0 / (1.0 + jnp.exp(-x))


def _silu(x):
    return x * _sigmoid(x)


def _layernorm_rows(y, g, b):
    mu = jnp.mean(y, axis=-1, keepdims=True)
    yc = y - mu
    var = jnp.mean(yc * yc, axis=-1, keepdims=True)
    return yc * lax.rsqrt(var + 1e-5) * g + b


def _ln_kernel(x_ref, g_ref, b_ref, o_ref):
    o_ref[...] = _layernorm_rows(x_ref[...], g_ref[...], b_ref[...])


def _input_layernorm(x, g, b):
    t, d = x.shape
    row = pl.BlockSpec((TOKEN_TILE, d), lambda i: (i, 0))
    vec = pl.BlockSpec((1, d), lambda i: (0, 0))
    return pl.pallas_call(
        _ln_kernel, grid=(t // TOKEN_TILE,), in_specs=[row, vec, vec], out_specs=row,
        out_shape=jax.ShapeDtypeStruct((t, d), F32), compiler_params=_cparams("parallel"),
        name="input_layernorm")(x, g.reshape(1, d), b.reshape(1, d))


def _proj_kernel(h_ref, w_ref, ws_ref, cos_ref, sin_ref, qg_ref, kg_ref, seg_ref, gp_ref,
                 q_ref, k_ref, v_ref, bq_ref, bk_ref, bv_ref, c_ref, z_ref, sm_ref):
    x = h_ref[...].astype(BF16)
    tm = x.shape[0]
    lane = lax.broadcasted_iota(jnp.int32, (tm, LANES), 1)
    first_half = (lane % (HEAD_DIM // 2)) < (HEAD_DIM // 4)
    low_head = lane < HEAD_DIM
    cos = cos_ref[...]
    sin = sin_ref[...]
    seg = seg_ref[...]

    def mm(a, b):
        return _dot(x, w_ref[:, a:b])

    def norm_rope(y, g):
        ms = _seg_sum(y * y, seg) * (1.0 / HEAD_DIM)
        yn = y * lax.rsqrt(ms + 1e-6) * g
        partner = jnp.where(first_half, pltpu.roll(yn, LANES - HEAD_DIM // 4, 1),
                            pltpu.roll(yn, HEAD_DIM // 4, 1))
        return yn * cos + partner * sin

    for j in range(A_Q_W // LANES):
        y = norm_rope(mm(j * LANES, (j + 1) * LANES), qg_ref[...]) * (LOG2_E * HEAD_DIM ** -0.5)
        yr = pltpu.roll(y, HEAD_DIM, 1)
        kv_head = (2 * j) // (A_Q_HEADS // A_KV_HEADS)
        if kv_head == 0:
            even, odd = jnp.where(low_head, y, 0.0), jnp.where(low_head, yr, 0.0)
        else:
            even, odd = jnp.where(low_head, 0.0, yr), jnp.where(low_head, 0.0, y)
        q_ref[:, (2 * j) * LANES:(2 * j + 1) * LANES] = even.astype(BF16)
        q_ref[:, (2 * j + 1) * LANES:(2 * j + 2) * LANES] = odd.astype(BF16)
    o = A_Q_W
    k_ref[...] = norm_rope(mm(o, o + A_KV_W), kg_ref[...]).astype(BF16)
    o += A_KV_W
    v_ref[:, 0:A_KV_W] = mm(o, o + A_KV_W).astype(BF16)
    v_ref[:, A_KV_W:2 * A_KV_W] = jnp.ones((tm, A_KV_W), BF16)
    o += A_KV_W
    bq_ref[...] = (mm(o, o + B_W) * (HEAD_DIM ** -0.5)).astype(BF16)
    o += B_W
    bk_ref[...] = mm(o, o + B_W).astype(BF16)
    o += B_W
    bv_ref[...] = mm(o, o + B_W).astype(BF16)
    o += B_W
    c_ref[...] = mm(o, o + 3 * C_W)
    o += 3 * C_W
    z_ref[...] = mm(o, o + C_W).astype(BF16)

    s = _dot(x, ws_ref[...])
    a = jnp.exp(gp_ref[0:1, :])
    u = s + gp_ref[1:2, :]
    softplus = jnp.maximum(u, 0.0) + jnp.log(1.0 + jnp.exp(-jnp.abs(u)))
    sm_ref[...] = jnp.where(lane < 2 * C_HEADS, _sigmoid(s), jnp.where(lane < SMALL_W, -a * softplus, 0.0))


def _projection(h, w_main, w_small, cos, sin, qg, kg, seg, gp, seq):
    t, d = h.shape
    tm = TOKEN_TILE
    nseq = seq // tm

    def rows(w):
        return pl.BlockSpec((tm, w), lambda i: (i, 0))

    def const(shape):
        return pl.BlockSpec(shape, lambda i: (0, 0))

    pos = pl.BlockSpec((tm, LANES), lambda i: (i % nseq, 0))
    out_w = [(2 * A_Q_W, BF16), (A_KV_W, BF16), (2 * A_KV_W, BF16), (B_W, BF16), (B_W, BF16), (B_W, BF16),
             (3 * C_W, F32), (C_W, BF16), (LANES, F32)]
    return pl.pallas_call(
        _proj_kernel, grid=(t // tm,),
        in_specs=[rows(d), const((d, MAIN_W)), const((d, LANES)), pos, pos, const((1, LANES)),
                  const((1, LANES)), const((LANES, LANES)), const((8, LANES))],
        out_specs=[rows(w) for w, _ in out_w],
        out_shape=[jax.ShapeDtypeStruct((t, w), dt) for w, dt in out_w],
        compiler_params=_cparams("parallel"), name="projection",
    )(h, w_main, w_small, cos, sin, qg, kg, seg, gp)


def _gqa_kernel(q_ref, k_ref, v1_ref, o_ref):
    tq = q_ref.shape[0]
    lane = lax.broadcasted_iota(jnp.int32, (tq, LANES), 1)
    low_head = lane < HEAD_DIM
    k = k_ref[...]
    v1 = v1_ref[...]
    group = A_Q_HEADS // A_KV_HEADS
    outs = []
    for h in range(A_Q_HEADS):
        s = _dot_nt(q_ref[:, h * LANES:(h + 1) * LANES], k)
        p = jnp.exp2(s - jnp.max(s, axis=-1, keepdims=True)).astype(BF16)
        acc = _dot(p, v1)
        outs.append(acc[:, 0:A_KV_W] / acc[:, A_KV_W:A_KV_W + 1])
    for j in range(A_Q_HEADS // 2):
        even, odd = outs[2 * j], outs[2 * j + 1]
        if (2 * j) // group == 0:
            blk = jnp.where(low_head, even, pltpu.roll(odd, HEAD_DIM, 1))
        else:
            blk = jnp.where(low_head, pltpu.roll(even, HEAD_DIM, 1), odd)
        o_ref[:, j * LANES:(j + 1) * LANES] = blk.astype(BF16)


def _gqa(qpad, k, v1, batch, seq):
    t = qpad.shape[0]
    tq = ATTN_Q_TILE
    nq = seq // tq
    return pl.pallas_call(
        _gqa_kernel, grid=(batch, nq),
        in_specs=[pl.BlockSpec((tq, 2 * A_Q_W), lambda b, i: (b * nq + i, 0)),
                  pl.BlockSpec((seq, A_KV_W), lambda b, i: (b, 0)),
                  pl.BlockSpec((seq, 2 * A_KV_W), lambda b, i: (b, 0))],
        out_specs=pl.BlockSpec((tq, A_Q_W), lambda b, i: (b * nq + i, 0)),
        out_shape=jax.ShapeDtypeStruct((t, A_Q_W), BF16),
        compiler_params=_cparams("parallel", "parallel"), name="gqa_attention",
    )(qpad, k, v1)


def _na_key_row0(qb, rows):
    return min(max(NA_ROWS_PER_BLOCK * qb - WIN_R // 2, 0), rows - 2 * NA_ROWS_PER_BLOCK)


def _na_kernel(q_ref, k_ref, v_ref, bias_ref, o_ref, *, rows):
    qb = pl.program_id(0)
    nq = q_ref.shape[0]
    nk = 2 * NA_ROWS_PER_BLOCK * GRID_W
    row0 = jnp.clip(NA_ROWS_PER_BLOCK * qb - WIN_R // 2, 0, rows - 2 * NA_ROWS_PER_BLOCK)
    start = pl.multiple_of(row0 * GRID_W, 4 * GRID_W)
    lane = lax.broadcasted_iota(jnp.int32, (nq, LANES), 1)
    low_head = lane < HEAD_DIM
    zero = jnp.zeros((), BF16)
    for j in range(B_W // LANES):
        kblk = k_ref[pl.ds(start, nk), j * LANES:(j + 1) * LANES]
        vblk = v_ref[pl.ds(start, nk), j * LANES:(j + 1) * LANES]
        qblk = q_ref[:, j * LANES:(j + 1) * LANES]
        outs = []
        for par in range(2):
            qh = jnp.where(low_head if par == 0 else jnp.logical_not(low_head), qblk, zero)
            s = _dot_nt(qh, kblk) + bias_ref[0, 2 * j + par]
            m = jnp.max(s, axis=-1, keepdims=True)
            p = jnp.exp(s - m)
            l = jnp.sum(p, axis=-1, keepdims=True)
            outs.append(_dot(p.astype(BF16), vblk) / l)
        o_ref[:, j * LANES:(j + 1) * LANES] = jnp.where(low_head, outs[0], outs[1]).astype(BF16)


def _na_bias_table(rpb, rows):
    nqb = rows // NA_ROWS_PER_BLOCK
    nkr = 2 * NA_ROWS_PER_BLOCK
    wr = min(WIN_R, rows)
    n_dr = 2 * WIN_R - 1
    c = np.arange(GRID_W)
    c0 = np.clip(c - WIN_C // 2, 0, GRID_W - WIN_C)
    in_c = (c[None, :] >= c0[:, None]) & (c[None, :] < c0[:, None] + WIN_C)
    dc = np.clip(c[None, :] - c[:, None] + (WIN_C - 1), 0, 2 * WIN_C - 2)
    col_tiles = jnp.where(jnp.asarray(in_c)[None, None], rpb[:, :, jnp.asarray(dc)], NEG_BIG)
    col_tiles = jnp.concatenate([col_tiles, jnp.full((B_HEADS, 1, GRID_W, GRID_W), NEG_BIG, F32)], axis=1)
    tile_idx = np.full((nqb, NA_ROWS_PER_BLOCK, nkr), n_dr, np.int32)
    for qb in range(nqb):
        for rq in range(NA_ROWS_PER_BLOCK):
            r = NA_ROWS_PER_BLOCK * qb + rq
            r0 = min(max(r - wr // 2, 0), rows - wr)
            for rk in range(nkr):
                kr = _na_key_row0(qb, rows) + rk
                if r0 <= kr < r0 + wr:
                    tile_idx[qb, rq, rk] = kr - r + (WIN_R - 1)
    table = jnp.take(col_tiles, jnp.asarray(tile_idx.reshape(-1)), axis=1)
    table = table.reshape(B_HEADS, nqb, NA_ROWS_PER_BLOCK, nkr, GRID_W, GRID_W)
    table = jnp.transpose(table, (1, 0, 2, 4, 3, 5))
    return table.reshape(nqb, B_HEADS, NA_ROWS_PER_BLOCK * GRID_W, nkr * GRID_W)


def _neighbourhood(q, k, v, bias, batch, seq):
    t = q.shape[0]
    rows = seq // GRID_W
    nqb = rows // NA_ROWS_PER_BLOCK
    nq = NA_ROWS_PER_BLOCK * GRID_W
    kv = pl.BlockSpec((seq, B_W), lambda qb, b: (b, 0))
    qo = pl.BlockSpec((nq, B_W), lambda qb, b: (b * nqb + qb, 0))
    return pl.pallas_call(
        functools.partial(_na_kernel, rows=rows), grid=(nqb, batch),
        in_specs=[qo, kv, kv, pl.BlockSpec((1, B_HEADS, nq, 2 * nq), lambda qb, b: (qb, 0, 0, 0))],
        out_specs=qo, out_shape=jax.ShapeDtypeStruct((t, B_W), BF16),
        compiler_params=_cparams("parallel", "parallel"), name="neighbourhood_attention",
    )(q, k, v, bias)


def _dot_select(x, sel):
    hi = x.astype(BF16)
    r1 = x - hi.astype(F32)
    mid = r1.astype(BF16)
    lo = (r1 - mid.astype(F32)).astype(BF16)
    return _dot(hi, sel) + _dot(mid, sel) + _dot(lo, sel)


def _conv_kernel(c_ref, prev_ref, next_ref, w_ref, seg_ref, sm_ref, e_ref, cum_f_ref, cum_b_ref,
                 q_ref, k_ref, kbf_ref, kbb_ref, vbf_ref, vbb_ref, gcf_ref, gcb_ref, pad_ref, *, tiles_per_seq):
    i = pl.program_id(0)
    tm = c_ref.shape[0]
    halo = prev_ref.shape[0]
    first = (i % tiles_per_seq) == 0
    last = (i % tiles_per_seq) == tiles_per_seq - 1
    pad_ref[0:halo, :] = jnp.where(first, 0.0, prev_ref[...])
    pad_ref[halo:halo + tm, :] = c_ref[...]
    pad_ref[halo + tm:2 * halo + tm, :] = jnp.where(last, 0.0, next_ref[...])
    acc = None
    for j in range(CONV_K):
        o = halo - CONV_K // 2 + j
        term = w_ref[j:j + 1, :] * pad_ref[o:o + tm, :]
        acc = term if acc is None else acc + term
    y = _silu(acc)
    seg = seg_ref[...]

    def l2n(u):
        return u * lax.rsqrt(_seg_sum(u * u, seg) + 1e-6)

    q_ref[...] = (l2n(y[:, 0:C_W]) * (HEAD_DIM ** -0.5)).astype(BF16)
    k = l2n(y[:, C_W:2 * C_W])
    v = y[:, 2 * C_W:3 * C_W]
    k_ref[...] = k.astype(BF16)

    sm = sm_ref[...]
    beta_f = _dot_select(sm, e_ref[:, 0 * C_W:1 * C_W])
    beta_b = _dot_select(sm, e_ref[:, 1 * C_W:2 * C_W])
    kbf_ref[...] = (k * beta_f).astype(BF16)
    kbb_ref[...] = (k * beta_b).astype(BF16)
    vbf_ref[...] = (v * beta_f).astype(BF16)
    vbb_ref[...] = (v * beta_b).astype(BF16)
    gcf_ref[...] = _dot_select_lhs(cum_f_ref[...], _dot_select(sm, e_ref[:, 2 * C_W:3 * C_W]))
    gcb_ref[...] = _dot_select_lhs(cum_b_ref[...], _dot_select(sm, e_ref[:, 3 * C_W:4 * C_W]))


def _dot_select_lhs(sel, x):
    hi = x.astype(BF16)
    r1 = x - hi.astype(F32)
    mid = r1.astype(BF16)
    lo = (r1 - mid.astype(F32)).astype(BF16)
    return _dot(sel, hi) + _dot(sel, mid) + _dot(sel, lo)


def _short_conv_norm(c, sm, conv_w, seg, seq):
    t, w = c.shape
    tm = TOKEN_TILE
    halo = 8
    nb = tm // halo
    last_blk = t // halo - 1
    e, cum_f, cum_b = _gdn_constants(tm)
    out = pl.BlockSpec((tm, C_W), lambda i: (i, 0))

    def const(a):
        return pl.BlockSpec(a.shape, lambda i: (0, 0))

    return pl.pallas_call(
        functools.partial(_conv_kernel, tiles_per_seq=seq // tm), grid=(t // tm,),
        in_specs=[pl.BlockSpec((tm, w), lambda i: (i, 0)),
                  pl.BlockSpec((halo, w), lambda i: (jnp.maximum(i * nb - 1, 0), 0)),
                  pl.BlockSpec((halo, w), lambda i: (jnp.minimum((i + 1) * nb, last_blk), 0)),
                  pl.BlockSpec((8, w), lambda i: (0, 0)),
                  pl.BlockSpec((C_W, C_W), lambda i: (0, 0)),
                  pl.BlockSpec((tm, LANES), lambda i: (i, 0)), const(e), const(cum_f), const(cum_b)],
        out_specs=[out] * 8,
        out_shape=[jax.ShapeDtypeStruct((t, C_W), BF16)] * 6 + [jax.ShapeDtypeStruct((t, C_W), F32)] * 2,
        scratch_shapes=[pltpu.VMEM((tm + 2 * halo, w), F32)],
        compiler_params=_cparams("parallel"), name="gdn_conv_norm",
    )(c, c, c, conv_w, seg, sm, e, cum_f, cum_b)


def _block_diag(x, same_head):
    return jnp.where(same_head, jnp.concatenate([x] * C_HEADS, axis=0), 0.0)


def _gdn_prepare(chains):
    C = GDN_CHUNK
    ri = lax.broadcasted_iota(jnp.int32, (C, C_W), 0)
    lj = lax.broadcasted_iota(jnp.int32, (C, C_W), 1) % C
    rr = lax.broadcasted_iota(jnp.int32, (C_HEADS * C, C_W), 0) // C
    ll = lax.broadcasted_iota(jnp.int32, (C_HEADS * C, C_W), 1) // HEAD_DIM
    same_head = rr == ll
    eye = jnp.where(ri == lj, 1.0, 0.0)

    def bd(x):
        return _block_diag(x.astype(BF16), same_head)

    n = range(len(chains))
    decay = [jnp.exp(jnp.where((ri <= lj) if ch["reverse"] else (ri >= lj), ch["gc"] - ch["grow"], -jnp.inf))
             for ch in chains]
    r = [_dot_nt(jnp.concatenate([ch["kb"], ch["q"]], axis=0), bd(ch["k"])) for ch in chains]
    m = [jnp.where((ri < lj) if chains[i]["reverse"] else (ri > lj), r[i][:C] * decay[i], 0.0) for i in n]
    qk = [(r[i][C:] * decay[i]).astype(BF16) for i in n]

    p = [eye - m[i] for i in n]
    a = [_dot(m[i].astype(BF16), bd(m[i])) for i in n]
    for _ in range(int(math.log2(C)) - 2):
        r2 = [_dot(jnp.concatenate([a[i], p[i]], axis=0).astype(BF16), bd(a[i])) for i in n]
        a = [r2[i][:C] for i in n]
        p = [p[i] + r2[i][C:] for i in n]
    t_inv = [(p[i] + _dot(p[i].astype(BF16), bd(a[i]))).astype(BF16) for i in n]

    egc = [jnp.exp(ch["gc"]) for ch in chains]
    uw = [_dot(t_inv[i], jnp.concatenate([bd(chains[i]["vb"]), bd(chains[i]["kb"].astype(F32) * egc[i])], axis=1))
          for i in n]
    out = []
    for i in n:
        ch = chains[i]
        gc = ch["gc"]
        g_last = gc[0:1, :] if ch["reverse"] else gc[C - 1:C, :]
        ke = (ch["k"].astype(F32) * jnp.exp(g_last - gc)).astype(BF16)
        ab = _dot_tn(ke, uw[i].astype(BF16))
        u, w = uw[i][:, :C_W], uw[i][:, C_W:]
        qo = _dot(qk[i], jnp.concatenate([bd(u), bd(w)], axis=1))
        out.append(dict(
            a=jnp.where(same_head, -ab[:, C_W:], 0.0).astype(BF16),
            b=jnp.where(same_head, ab[:, :C_W], 0.0),
            d=jnp.exp(g_last),
            qeff=(ch["q"].astype(F32) * egc[i] - qo[:, C_W:]).astype(BF16),
            oloc=qo[:, :C_W]))
    return out


GDN_PREP_CHUNKS = 2


def _gdn_kernel(qf, kf, kbf, vbf, gcf, gwf, qb, kb, kbb, vbb, gcb, gwb, of_ref, ob_ref,
                state_ref, a_ref, b_ref, d_ref, qeff_ref, oloc_ref):
    C = GDN_CHUNK
    n = GDN_BLOCK // C
    dirs = ((qf, kf, kbf, vbf, gcf, gwf, False), (qb, kb, kbb, vbb, gcb, gwb, True))

    @pl.when(pl.program_id(1) == 0)
    def _():
        state_ref[...] = jnp.zeros_like(state_ref)

    def prepare(it, carry):
        chains, where = [], []
        for d, (q, k, kbeta, vbeta, gc, gw, reverse) in enumerate(dirs):
            for j in range(GDN_PREP_CHUNKS):
                c = it * GDN_PREP_CHUNKS + j
                sl = pl.ds(pl.multiple_of(c * C, C), C)
                chains.append(dict(q=q[sl, :], k=k[sl, :], kb=kbeta[sl, :], vb=vbeta[sl, :], gc=gc[sl, :],
                                   grow=gw[0, pl.ds(c, 1), :], reverse=reverse))
                where.append((d, c))
        for (d, c), res in zip(where, _gdn_prepare(chains)):
            a_ref[d, c] = res["a"]
            b_ref[d, c] = res["b"]
            d_ref[d, c] = res["d"]
            qeff_ref[d, c] = res["qeff"]
            oloc_ref[d, c] = res["oloc"]
        return carry

    lax.fori_loop(0, n // GDN_PREP_CHUNKS, prepare, 0)

    def scan(step, carry):
        cs = (step, n - 1 - step)
        states = [state_ref[d] for d in range(2)]
        lhs = [jnp.concatenate([a_ref[d, cs[d]], qeff_ref[d, cs[d]]], axis=0) for d in range(2)]
        res = [_dot(lhs[d], states[d].astype(BF16)) for d in range(2)]
        for d, o_ref in enumerate((of_ref, ob_ref)):
            state_ref[d] = states[d] * d_ref[d, cs[d]] + res[d][:C_W] + b_ref[d, cs[d]]
            o_ref[pl.ds(pl.multiple_of(cs[d] * C, C), C), :] = res[d][C_W:] + oloc_ref[d, cs[d]]
        return carry

    lax.fori_loop(0, n, scan, 0)


def _gdn_constants(tile):
    C = GDN_CHUNK
    e = np.zeros((LANES, 4 * C_W), np.float32)
    for kind in range(4):
        for h in range(C_HEADS):
            e[kind * C_HEADS + h, kind * C_W + h * HEAD_DIM:kind * C_W + (h + 1) * HEAD_DIM] = 1.0
    i = np.arange(C)
    tril = (i[None, :] <= i[:, None]).astype(np.float32)
    cum_f = np.kron(np.eye(tile // C, dtype=np.float32), tril)
    cum_b = np.kron(np.eye(tile // C, dtype=np.float32), tril.T)
    return tuple(jnp.asarray(a, BF16) for a in (e, cum_f, cum_b))


def _gated_delta(q, k, kb_f, kb_b, vb_f, vb_b, gc_f, gc_b, batch, seq):
    t = q.shape[0]
    nb = seq // GDN_BLOCK
    nc = GDN_BLOCK // GDN_CHUNK

    def rows(gc):
        g = gc[:, ::HEAD_DIM].reshape(t // GDN_CHUNK, GDN_CHUNK, C_HEADS)
        return jnp.transpose(g, (0, 2, 1)).reshape(t // GDN_BLOCK, nc, C_W)

    def fwd(w):
        return pl.BlockSpec((GDN_BLOCK, w), lambda b, s: (b * nb + s, 0))

    def bwd(w):
        return pl.BlockSpec((GDN_BLOCK, w), lambda b, s: (b * nb + nb - 1 - s, 0))

    gf_spec = pl.BlockSpec((1, nc, C_W), lambda b, s: (b * nb + s, 0, 0))
    gb_spec = pl.BlockSpec((1, nc, C_W), lambda b, s: (b * nb + nb - 1 - s, 0, 0))
    return pl.pallas_call(
        _gdn_kernel, grid=(batch, nb),
        in_specs=[fwd(C_W)] * 5 + [gf_spec] + [bwd(C_W)] * 5 + [gb_spec],
        out_specs=[fwd(C_W), bwd(C_W)],
        out_shape=[jax.ShapeDtypeStruct((t, C_W), F32)] * 2,
        scratch_shapes=[pltpu.VMEM((2, C_W, C_W), F32),
                        pltpu.VMEM((2, nc, C_W, C_W), BF16), pltpu.VMEM((2, nc, C_W, C_W), F32),
                        pltpu.VMEM((2, nc, 1, C_W), F32),
                        pltpu.VMEM((2, nc, GDN_CHUNK, C_W), BF16), pltpu.VMEM((2, nc, GDN_CHUNK, C_W), F32)],
        compiler_params=_cparams("parallel", "arbitrary"), name="gated_delta_rule",
    )(q, k, kb_f, vb_f, gc_f, rows(gc_f), q, k, kb_b, vb_b, gc_b, rows(gc_b))


def _merge_kernel(h_ref, ya_ref, yb_ref, of_ref, ob_ref, z_ref, wg_ref, wa_ref, wb_ref, wc_ref, wo_ref,
                  gn_ref, seg_ref, lg_ref, lb_ref, wrh_ref, wrl_ref, rb_ref, h1_ref, route_ref):
    h = h_ref[...]
    x = h.astype(BF16)
    tm = h.shape[0]
    d = D_MODEL
    o = of_ref[...] + ob_ref[...]
    ms = _seg_sum(o * o, seg_ref[...]) * (1.0 / HEAD_DIM)
    yc = o * lax.rsqrt(ms + 1e-6) * gn_ref[...] * _silu(z_ref[...].astype(F32))
    mix = _sigmoid(_dot(x, wg_ref[:, 0:d])) * _dot(ya_ref[...], wa_ref[...])
    mix = mix + _sigmoid(_dot(x, wg_ref[:, d:2 * d])) * _dot(yb_ref[...], wb_ref[...])
    mix = mix + _sigmoid(_dot(x, wg_ref[:, 2 * d:3 * d])) * _dot(yc.astype(BF16), wc_ref[...])
    y = DN_ALPHA * h + _dot(mix.astype(BF16), wo_ref[...])
    h1 = _layernorm_rows(y, lg_ref[...], lb_ref[...])
    h1_ref[...] = h1

    hi = h1.astype(BF16)
    lo = (h1 - hi.astype(F32)).astype(BF16)
    logits = _dot(hi, wrh_ref[...]) + _dot(lo, wrh_ref[...]) + _dot(hi, wrl_ref[...])
    scores = _sigmoid(logits)
    sel = scores + rb_ref[...]
    lane = lax.broadcasted_iota(jnp.int32, (tm, LANES), 1)
    in_grp = lane < N_GROUPS
    sel_m = [sel if m == 0 else pltpu.roll(sel, LANES - m * N_GROUPS, 1) for m in range(GROUP_SIZE)]
    sc_m = [scores if m == 0 else pltpu.roll(scores, LANES - m * N_GROUPS, 1) for m in range(GROUP_SIZE)]
    hi01, lo01 = jnp.maximum(sel_m[0], sel_m[1]), jnp.minimum(sel_m[0], sel_m[1])
    hi23, lo23 = jnp.maximum(sel_m[2], sel_m[3]), jnp.minimum(sel_m[2], sel_m[3])
    top1 = jnp.maximum(hi01, hi23)
    top2 = jnp.maximum(jnp.minimum(hi01, hi23), jnp.maximum(lo01, lo23))
    grp_score = jnp.where(in_grp, top1 + top2, -jnp.inf)
    best_score = jnp.max(grp_score, axis=-1, keepdims=True)
    best = jnp.min(jnp.where(grp_score == best_score, lane, LANES), axis=-1, keepdims=True)
    pick = lane == best
    val = [jnp.sum(jnp.where(pick, s, 0.0), axis=-1, keepdims=True) for s in sel_m]
    aff = [jnp.sum(jnp.where(pick, s, 0.0), axis=-1, keepdims=True) for s in sc_m]
    e_sel = [jnp.zeros((tm, 1), F32)] * TOP_K
    w_sel = [jnp.zeros((tm, 1), F32)] * TOP_K
    for m in range(GROUP_SIZE):
        rank = jnp.zeros((tm, 1), jnp.int32)
        for j in range(GROUP_SIZE):
            if j != m:
                ahead = (val[j] > val[m]) | ((val[j] == val[m]) & (j < m))
                rank = rank + ahead.astype(jnp.int32)
        for kk in range(TOP_K):
            e_sel[kk] = jnp.where(rank == kk, (best * GROUP_SIZE + m).astype(F32), e_sel[kk])
            w_sel[kk] = jnp.where(rank == kk, aff[m], w_sel[kk])
    tot = w_sel[0] + w_sel[1]
    route_ref[...] = jnp.where(lane == 0, e_sel[0], jnp.where(lane == 1, e_sel[1], jnp.where(
        lane == 2, w_sel[0] / tot, jnp.where(lane == 3, w_sel[1] / tot, 0.0))))


def _merge(h, ya, yb, o_f, o_b, z, wg, wa, wb, wc, wo, gn, seg, lg, lb, wr_hi, wr_lo, rb):
    t, d = h.shape
    tm = TOKEN_TILE

    def rows(w):
        return pl.BlockSpec((tm, w), lambda i: (i, 0))

    def const(a):
        return pl.BlockSpec(a.shape, lambda i: (0, 0))

    consts = (wg, wa, wb, wc, wo, gn, seg, lg, lb, wr_hi, wr_lo, rb)
    return pl.pallas_call(
        _merge_kernel, grid=(t // tm,),
        in_specs=[rows(d), rows(A_Q_W), rows(B_W), rows(C_W), rows(C_W), rows(C_W)] + [const(a) for a in consts],
        out_specs=[rows(d), rows(LANES)],
        out_shape=[jax.ShapeDtypeStruct((t, d), F32), jax.ShapeDtypeStruct((t, LANES), F32)],
        compiler_params=_cparams("parallel"), name="merge_outproj_ln_router",
    )(h, ya, yb, o_f, o_b, z, *consts)


def _expert_kernel(te_ref, nv_ref, x_ref, w1_ref, w3_ref, w2_ref, y_ref, w1b_ref, w3b_ref, w2b_ref):
    i = pl.program_id(0)

    @pl.when((i == 0) | (te_ref[i] != te_ref[jnp.maximum(i - 1, 0)]))
    def _():
        w1b_ref[...] = w1_ref[0].astype(BF16)
        w3b_ref[...] = w3_ref[0].astype(BF16)
        w2b_ref[...] = w2_ref[0].astype(BF16)

    @pl.when(i < nv_ref[0])
    def _():
        x = x_ref[...]
        hid = _silu(_dot(x, w1b_ref[...])) * _dot(x, w3b_ref[...])
        y_ref[...] = _dot(hid.astype(BF16), w2b_ref[...]).astype(BF16)

    @pl.when(i >= nv_ref[0])
    def _():
        y_ref[...] = jnp.zeros_like(y_ref)


def _expert_mlp(xs, tile_expert, n_valid, w1, w3, w2):
    mp, d = xs.shape
    te = EXPERT_TILE
    grid_spec = pltpu.PrefetchScalarGridSpec(
        num_scalar_prefetch=2, grid=(mp // te,),
        in_specs=[pl.BlockSpec((te, d), lambda i, e, n: (i, 0)),
                  pl.BlockSpec((1, d, D_EXPERT), lambda i, e, n: (e[i], 0, 0)),
                  pl.BlockSpec((1, d, D_EXPERT), lambda i, e, n: (e[i], 0, 0)),
                  pl.BlockSpec((1, D_EXPERT, d), lambda i, e, n: (e[i], 0, 0))],
        out_specs=pl.BlockSpec((te, d), lambda i, e, n: (i, 0)),
        scratch_shapes=[pltpu.VMEM((d, D_EXPERT), BF16), pltpu.VMEM((d, D_EXPERT), BF16),
                        pltpu.VMEM((D_EXPERT, d), BF16)])
    return pl.pallas_call(
        _expert_kernel, grid_spec=grid_spec, out_shape=jax.ShapeDtypeStruct((mp, d), BF16),
        compiler_params=_cparams("arbitrary"), name="expert_mlp",
    )(tile_expert, n_valid, xs, w1, w3, w2)


def _dispatch_plan(route, t):
    te = EXPERT_TILE
    e = route[:, 0:TOP_K].astype(jnp.int32)
    onehot = (e[:, :, None] == jnp.arange(N_EXPERTS, dtype=jnp.int32)[None, None, :]).astype(jnp.int32).sum(axis=1)
    csum = jnp.cumsum(onehot, axis=0)
    counts = csum[-1]
    padded = ((counts + te - 1) // te) * te
    ends = jnp.cumsum(padded)
    offsets = ends - padded
    rank = jnp.take_along_axis(csum, e, axis=1) - 1
    pos = offsets[e] + rank
    n_tiles = (t * TOP_K) // te + N_EXPERTS
    tile_start = jnp.arange(n_tiles, dtype=jnp.int32) * te
    tile_expert = jnp.minimum((tile_start[:, None] >= ends[None, :]).astype(jnp.int32).sum(axis=1), N_EXPERTS - 1)
    n_valid = (ends[-1] // te).astype(jnp.int32).reshape(1)
    src = jnp.zeros((n_tiles * te,), jnp.int32).at[pos.reshape(-1)].set(
        jnp.repeat(jnp.arange(t, dtype=jnp.int32), TOP_K))
    return pos, src, tile_expert, n_valid


def _combine_kernel(h_ref, y0_ref, y1_ref, route_ref, g_ref, b_ref, o_ref):
    lane = lax.broadcasted_iota(jnp.int32, route_ref.shape, 1)
    r = route_ref[...]
    w0 = jnp.sum(jnp.where(lane == 2, r, 0.0), axis=-1, keepdims=True)
    w1 = jnp.sum(jnp.where(lane == 3, r, 0.0), axis=-1, keepdims=True)
    y = w0 * y0_ref[...].astype(F32) + w1 * y1_ref[...].astype(F32)
    o_ref[...] = _layernorm_rows(DN_ALPHA * h_ref[...] + y, g_ref[...], b_ref[...])


def _combine(h1, y0, y1, route, g, b):
    t, d = h1.shape
    row = pl.BlockSpec((TOKEN_TILE, d), lambda i: (i, 0))
    vec = pl.BlockSpec((1, d), lambda i: (0, 0))
    return pl.pallas_call(
        _combine_kernel, grid=(t // TOKEN_TILE,),
        in_specs=[row, row, row, pl.BlockSpec((TOKEN_TILE, LANES), lambda i: (i, 0)), vec, vec],
        out_specs=row, out_shape=jax.ShapeDtypeStruct((t, d), F32),
        compiler_params=_cparams("parallel"), name="combine_ln",
    )(h1, y0, y1, route, g.reshape(1, d), b.reshape(1, d))


def _rope_tables(seq):
    tkn = np.arange(seq)
    pos = np.stack([tkn // GRID_W, tkn % GRID_W], axis=1).astype(np.float32)
    half = HEAD_DIM // 2
    inv = ROPE_THETA ** (-np.arange(0, half, 2, dtype=np.float32) / half)
    d = np.arange(LANES) % HEAD_DIM
    axis = d // half
    r = d % half
    ang = pos[:, axis] * inv[r % (half // 2)][None, :]
    sign = np.where(r < half // 2, -1.0, 1.0)[None, :]
    return jnp.asarray(np.cos(ang), F32), jnp.asarray(np.sin(ang) * sign, F32)


def _segment_matrix(width):
    i = np.arange(width) // HEAD_DIM
    return jnp.asarray((i[:, None] == i[None, :]).astype(np.float32), BF16)


def _router_layout(w_router, router_bias):
    perm = np.array([(l % N_GROUPS) * GROUP_SIZE + l // N_GROUPS for l in range(N_EXPERTS)])
    w = jnp.pad(w_router[:, perm], ((0, 0), (0, LANES - N_EXPERTS)))
    hi = w.astype(BF16)
    lo = (w - hi.astype(F32)).astype(BF16)
    rb = jnp.pad(router_bias[perm], (0, LANES - N_EXPERTS)).reshape(1, LANES)
    return hi, lo, rb


def kernel(x, ln0_g, ln0_b, w_in, q_norm_g, k_norm_g, na_rpb, conv_w, A_log, dt_bias, gdn_norm_g,
           w_branch_a, w_branch_b, w_branch_c, w_out, ln1_g, ln1_b, w_router, router_bias, w1, w3, w2,
           ln2_g, ln2_b):
    batch, seq, d = x.shape
    t = batch * seq
    depth = w_in.shape[0]
    rows = seq // GRID_W
    cos, sin = _rope_tables(seq)
    seg128 = _segment_matrix(LANES)
    seg256 = _segment_matrix(C_W)
    wr_hi, wr_lo, rb = _router_layout(w_router, router_bias)

    h = _input_layernorm(x.reshape(t, d), ln0_g, ln0_b)
    for l in range(depth):
        w_main = w_in[l, :, :MAIN_W].astype(BF16)
        w_small = jnp.pad(w_in[l, :, MAIN_W:MAIN_W + SMALL_W], ((0, 0), (0, LANES - SMALL_W))).astype(BF16)
        w_gate = w_in[l, :, MAIN_W + SMALL_W:].astype(BF16)
        gp = jnp.zeros((8, LANES), F32)
        gp = gp.at[0, 2 * C_HEADS:4 * C_HEADS].set(A_log[l].reshape(-1))
        gp = gp.at[1, 2 * C_HEADS:4 * C_HEADS].set(dt_bias[l].reshape(-1))
        qg = jnp.tile(q_norm_g[l], LANES // HEAD_DIM).reshape(1, LANES)
        kg = jnp.tile(k_norm_g[l], LANES // HEAD_DIM).reshape(1, LANES)
        qpad, ak, av, bq, bk, bv, cqkv, cz, sm = _projection(h, w_main, w_small, cos, sin, qg, kg, seg128, gp, seq)

        ya = _gqa(qpad, ak, av, batch, seq)
        yb = _neighbourhood(bq, bk, bv, _na_bias_table(na_rpb[l], rows), batch, seq)
        conv_pad = jnp.pad(conv_w[l], ((0, 8 - CONV_K), (0, 0)))
        cq, ck, kb_f, kb_b, vb_f, vb_b, gc_f, gc_b = _short_conv_norm(cqkv, sm, conv_pad, seg256, seq)
        o_f, o_b = _gated_delta(cq, ck, kb_f, kb_b, vb_f, vb_b, gc_f, gc_b, batch, seq)

        gn = jnp.tile(gdn_norm_g[l], C_HEADS).reshape(1, C_W)
        h1, route = _merge(h, ya, yb, o_f, o_b, cz, w_gate, w_branch_a[l].astype(BF16),
                           w_branch_b[l].astype(BF16), w_branch_c[l].astype(BF16), w_out[l].astype(BF16),
                           gn, seg256, ln1_g[l].reshape(1, d), ln1_b[l].reshape(1, d), wr_hi, wr_lo, rb)

        pos, src, tile_expert, n_valid = _dispatch_plan(route, t)
        xs = jnp.take(h1.astype(BF16), src, axis=0)
        ys = _expert_mlp(xs, tile_expert, n_valid, w1[l], w3[l], w2[l])
        y0 = jnp.take(ys, pos[:, 0], axis=0)
        y1 = jnp.take(ys, pos[:, 1], axis=0)
        h = _combine(h1, y0, y1, route, ln2_g[l], ln2_b[l])
    return h.reshape(batch, seq, d)
```

```python
import functools
import math

import numpy as np
import jax
import jax.numpy as jnp
from jax import lax
from jax.experimental import pallas as pl
from jax.experimental.pallas import tpu as pltpu

F32 = jnp.float32
BF16 = jnp.bfloat16

D_MODEL = 1024
GRID_W = 64
HEAD_DIM = 64
A_Q_HEADS = 8
A_KV_HEADS = 2
ROPE_THETA = 10000.0
B_HEADS = 4
WIN_R = 8
WIN_C = 16
C_HEADS = 4
CONV_K = 5
N_EXPERTS = 32
N_GROUPS = 8
GROUP_SIZE = N_EXPERTS // N_GROUPS
TOP_K = 2
D_EXPERT = 512
DEPTH = 4
DN_ALPHA = (2 * DEPTH) ** 0.25

A_Q_W = A_Q_HEADS * HEAD_DIM
A_KV_W = A_KV_HEADS * HEAD_DIM
B_W = B_HEADS * HEAD_DIM
C_W = C_HEADS * HEAD_DIM
MAIN_W = A_Q_W + 2 * A_KV_W + 3 * B_W + 4 * C_W
SMALL_W = 4 * C_HEADS

LANES = 128
TOKEN_TILE = 512
ATTN_Q_TILE = 512
NA_ROWS_PER_BLOCK = 8
GDN_CHUNK = 64
GDN_BLOCK = 512
EXPERT_TILE = 256
VMEM_LIMIT = 48 * 1024 * 1024
NEG_BIG = -1e30
LOG2_E = math.log2(math.e)
HIGHEST = lax.Precision.HIGHEST


def _cparams(*sem):
    return pltpu.CompilerParams(dimension_semantics=sem, vmem_limit_bytes=VMEM_LIMIT)


def _dot(a, b):
    return jnp.dot(a, b, preferred_element_type=F32)


def _dot_nt(a, b):
    return lax.dot_general(a, b, (((1,), (1,)), ((), ())), preferred_element_type=F32)


def _dot_tn(a, b):
    return lax.dot_general(a, b, (((0,), (0,)), ((), ())), preferred_element_type=F32)


def _dot_exact(a, b):
    return jnp.dot(a, b, preferred_element_type=F32, precision=HIGHEST)


def _seg_sum(y, seg):
    hi = y.astype(BF16)
    lo = (y - hi.astype(F32)).astype(BF16)
    return _dot(hi, seg) + _dot(lo, seg)


def _sigmoid(x):
    return 1.0 / (1.0 + jnp.exp(-x))


def _silu(x):
    return x * _sigmoid(x)


def _layernorm_rows(y, g, b):
    mu = jnp.mean(y, axis=-1, keepdims=True)
    yc = y - mu
    var = jnp.mean(yc * yc, axis=-1, keepdims=True)
    return yc * lax.rsqrt(var + 1e-5) * g + b


def _ln_kernel(x_ref, g_ref, b_ref, o_ref):
    o_ref[...] = _layernorm_rows(x_ref[...], g_ref[...], b_ref[...])


def _input_layernorm(x, g, b):
    t, d = x.shape
    row = pl.BlockSpec((TOKEN_TILE, d), lambda i: (i, 0))
    vec = pl.BlockSpec((1, d), lambda i: (0, 0))
    return pl.pallas_call(
        _ln_kernel, grid=(t // TOKEN_TILE,), in_specs=[row, vec, vec], out_specs=row,
        out_shape=jax.ShapeDtypeStruct((t, d), F32), compiler_params=_cparams("parallel"),
        name="input_layernorm")(x, g.reshape(1, d), b.reshape(1, d))


def _proj_kernel(h_ref, w_ref, ws_ref, cos_ref, sin_ref, qg_ref, kg_ref, seg_ref, gp_ref,
                 q_ref, k_ref, v_ref, bq_ref, bk_ref, bv_ref, c_ref, z_ref, sm_ref):
    x = h_ref[...].astype(BF16)
    tm = x.shape[0]
    lane = lax.broadcasted_iota(jnp.int32, (tm, LANES), 1)
    first_half = (lane % (HEAD_DIM // 2)) < (HEAD_DIM // 4)
    low_head = lane < HEAD_DIM
    cos = cos_ref[...]
    sin = sin_ref[...]
    seg = seg_ref[...]

    def mm(a, b):
        return _dot(x, w_ref[0, :, a:b])

    def norm_rope(y, g):
        ms = _seg_sum(y * y, seg) * (1.0 / HEAD_DIM)
        yn = y * lax.rsqrt(ms + 1e-6) * g
        partner = jnp.where(first_half, pltpu.roll(yn, LANES - HEAD_DIM // 4, 1),
                            pltpu.roll(yn, HEAD_DIM // 4, 1))
        return yn * cos + partner * sin

    for j in range(A_Q_W // LANES):
        y = norm_rope(mm(j * LANES, (j + 1) * LANES), qg_ref[...]) * (LOG2_E * HEAD_DIM ** -0.5)
        yr = pltpu.roll(y, HEAD_DIM, 1)
        kv_head = (2 * j) // (A_Q_HEADS // A_KV_HEADS)
        if kv_head == 0:
            even, odd = jnp.where(low_head, y, 0.0), jnp.where(low_head, yr, 0.0)
        else:
            even, odd = jnp.where(low_head, 0.0, yr), jnp.where(low_head, 0.0, y)
        q_ref[:, (2 * j) * LANES:(2 * j + 1) * LANES] = even.astype(BF16)
        q_ref[:, (2 * j + 1) * LANES:(2 * j + 2) * LANES] = odd.astype(BF16)
    o = A_Q_W
    k_ref[...] = norm_rope(mm(o, o + A_KV_W), kg_ref[...]).astype(BF16)
    o += A_KV_W
    v_ref[:, 0:A_KV_W] = mm(o, o + A_KV_W).astype(BF16)
    v_ref[:, A_KV_W:2 * A_KV_W] = jnp.ones((tm, A_KV_W), BF16)
    o += A_KV_W
    bq_ref[...] = (mm(o, o + B_W) * (HEAD_DIM ** -0.5)).astype(BF16)
    o += B_W
    bk_ref[...] = mm(o, o + B_W).astype(BF16)
    o += B_W
    bv_ref[...] = mm(o, o + B_W).astype(BF16)
    o += B_W
    c_ref[...] = mm(o, o + 3 * C_W)
    o += 3 * C_W
    z_ref[...] = mm(o, o + C_W).astype(BF16)

    s = _dot(x, ws_ref[0])
    a = jnp.exp(gp_ref[0:1, :])
    u = s + gp_ref[1:2, :]
    softplus = jnp.maximum(u, 0.0) + jnp.log(1.0 + jnp.exp(-jnp.abs(u)))
    sm_ref[...] = jnp.where(lane < 2 * C_HEADS, _sigmoid(s), jnp.where(lane < SMALL_W, -a * softplus, 0.0))


def _layer_block(a, layer):
    return pl.BlockSpec((1,) + a.shape[1:], lambda i: (layer,) + (0,) * (a.ndim - 1))


def _projection(h, w_main, w_small, layer, cos, sin, qg, kg, seg, gp, seq):
    t, d = h.shape
    tm = TOKEN_TILE
    nseq = seq // tm

    def rows(w):
        return pl.BlockSpec((tm, w), lambda i: (i, 0))

    def const(shape):
        return pl.BlockSpec(shape, lambda i: (0, 0))

    pos = pl.BlockSpec((tm, LANES), lambda i: (i % nseq, 0))
    out_w = [(2 * A_Q_W, BF16), (A_KV_W, BF16), (2 * A_KV_W, BF16), (B_W, BF16), (B_W, BF16), (B_W, BF16),
             (3 * C_W, F32), (C_W, BF16), (LANES, F32)]
    return pl.pallas_call(
        _proj_kernel, grid=(t // tm,),
        in_specs=[rows(d), _layer_block(w_main, layer), _layer_block(w_small, layer), pos, pos, const((1, LANES)),
                  const((1, LANES)), const((LANES, LANES)), const((8, LANES))],
        out_specs=[rows(w) for w, _ in out_w],
        out_shape=[jax.ShapeDtypeStruct((t, w), dt) for w, dt in out_w],
        compiler_params=_cparams("parallel"), name="projection",
    )(h, w_main, w_small, cos, sin, qg, kg, seg, gp)


def _gqa_kernel(q_ref, k_ref, v1_ref, o_ref):
    tq = q_ref.shape[0]
    lane = lax.broadcasted_iota(jnp.int32, (tq, LANES), 1)
    low_head = lane < HEAD_DIM
    k = k_ref[...]
    v1 = v1_ref[...]
    group = A_Q_HEADS // A_KV_HEADS
    outs = []
    for h in range(A_Q_HEADS):
        s = _dot_nt(q_ref[:, h * LANES:(h + 1) * LANES], k)
        p = jnp.exp2(s - jnp.max(s, axis=-1, keepdims=True)).astype(BF16)
        acc = _dot(p, v1)
        outs.append(acc[:, 0:A_KV_W] / acc[:, A_KV_W:A_KV_W + 1])
    for j in range(A_Q_HEADS // 2):
        even, odd = outs[2 * j], outs[2 * j + 1]
        if (2 * j) // group == 0:
            blk = jnp.where(low_head, even, pltpu.roll(odd, HEAD_DIM, 1))
        else:
            blk = jnp.where(low_head, pltpu.roll(even, HEAD_DIM, 1), odd)
        o_ref[:, j * LANES:(j + 1) * LANES] = blk.astype(BF16)


def _gqa(qpad, k, v1, batch, seq):
    t = qpad.shape[0]
    tq = ATTN_Q_TILE
    nq = seq // tq
    return pl.pallas_call(
        _gqa_kernel, grid=(batch, nq),
        in_specs=[pl.BlockSpec((tq, 2 * A_Q_W), lambda b, i: (b * nq + i, 0)),
                  pl.BlockSpec((seq, A_KV_W), lambda b, i: (b, 0)),
                  pl.BlockSpec((seq, 2 * A_KV_W), lambda b, i: (b, 0))],
        out_specs=pl.BlockSpec((tq, A_Q_W), lambda b, i: (b * nq + i, 0)),
        out_shape=jax.ShapeDtypeStruct((t, A_Q_W), BF16),
        compiler_params=_cparams("parallel", "parallel"), name="gqa_attention",
    )(qpad, k, v1)


def _na_key_row0(qb, rows):
    return min(max(NA_ROWS_PER_BLOCK * qb - WIN_R // 2, 0), rows - 2 * NA_ROWS_PER_BLOCK)


def _na_kernel(which_ref, q_ref, k_ref, v_ref, pair_ref, o_ref, bias_ref, *, rows):
    qb = pl.program_id(0)
    nq = q_ref.shape[0]
    nk = 2 * NA_ROWS_PER_BLOCK * GRID_W

    @pl.when(pl.program_id(1) == 0)
    def _():
        n = NA_ROWS_PER_BLOCK
        for rq in range(n):
            for rp in range(n):
                tile = which_ref[(qb * n + rq) * n + rp]
                for h in range(B_HEADS):
                    bias_ref[h, rq * GRID_W:(rq + 1) * GRID_W, rp * LANES:(rp + 1) * LANES] = pair_ref[h, tile]

    row0 = jnp.clip(NA_ROWS_PER_BLOCK * qb - WIN_R // 2, 0, rows - 2 * NA_ROWS_PER_BLOCK)
    start = pl.multiple_of(row0 * GRID_W, 4 * GRID_W)
    lane = lax.broadcasted_iota(jnp.int32, (nq, LANES), 1)
    low_head = lane < HEAD_DIM
    zero = jnp.zeros((), BF16)
    for j in range(B_W // LANES):
        kblk = k_ref[pl.ds(start, nk), j * LANES:(j + 1) * LANES]
        vblk = v_ref[pl.ds(start, nk), j * LANES:(j + 1) * LANES]
        qblk = q_ref[:, j * LANES:(j + 1) * LANES]
        outs = []
        for par in range(2):
            qh = jnp.where(low_head if par == 0 else jnp.logical_not(low_head), qblk, zero)
            s = _dot_nt(qh, kblk) + bias_ref[2 * j + par]
            m = jnp.max(s, axis=-1, keepdims=True)
            p = jnp.exp(s - m)
            l = jnp.sum(p, axis=-1, keepdims=True)
            outs.append(_dot(p.astype(BF16), vblk) / l)
        o_ref[:, j * LANES:(j + 1) * LANES] = jnp.where(low_head, outs[0], outs[1]).astype(BF16)


def _na_bias_tiles(rpb, rows):
    nqb = rows // NA_ROWS_PER_BLOCK
    nkr = 2 * NA_ROWS_PER_BLOCK
    wr = min(WIN_R, rows)
    n_dr = 2 * WIN_R - 1
    c = np.arange(GRID_W)
    c0 = np.clip(c - WIN_C // 2, 0, GRID_W - WIN_C)
    in_c = (c[None, :] >= c0[:, None]) & (c[None, :] < c0[:, None] + WIN_C)
    dc = np.clip(c[None, :] - c[:, None] + (WIN_C - 1), 0, 2 * WIN_C - 2)
    col_tiles = jnp.where(jnp.asarray(in_c)[None, None], rpb[:, :, jnp.asarray(dc)], NEG_BIG)
    col_tiles = jnp.concatenate([col_tiles, jnp.full((B_HEADS, 1, GRID_W, GRID_W), NEG_BIG, F32)], axis=1)
    tile_idx = np.full((nqb, NA_ROWS_PER_BLOCK, nkr), n_dr, np.int32)
    for qb in range(nqb):
        for rq in range(NA_ROWS_PER_BLOCK):
            r = NA_ROWS_PER_BLOCK * qb + rq
            r0 = min(max(r - wr // 2, 0), rows - wr)
            for rk in range(nkr):
                kr = _na_key_row0(qb, rows) + rk
                if r0 <= kr < r0 + wr:
                    tile_idx[qb, rq, rk] = kr - r + (WIN_R - 1)
    pairs, which = np.unique(tile_idx.reshape(-1, 2), axis=0, return_inverse=True)
    pair_tiles = jnp.concatenate([jnp.take(col_tiles, jnp.asarray(pairs[:, 0]), axis=1),
                                  jnp.take(col_tiles, jnp.asarray(pairs[:, 1]), axis=1)], axis=-1)
    return pair_tiles, jnp.asarray(which.reshape(-1).astype(np.int32))


def _neighbourhood(q, k, v, pair_tiles, which, batch, seq):
    t = q.shape[0]
    rows = seq // GRID_W
    nqb = rows // NA_ROWS_PER_BLOCK
    nq = NA_ROWS_PER_BLOCK * GRID_W
    kv = pl.BlockSpec((seq, B_W), lambda qb, b, w: (b, 0))
    qo = pl.BlockSpec((nq, B_W), lambda qb, b, w: (b * nqb + qb, 0))
    grid_spec = pltpu.PrefetchScalarGridSpec(
        num_scalar_prefetch=1, grid=(nqb, batch),
        in_specs=[qo, kv, kv, pl.BlockSpec(pair_tiles.shape, lambda qb, b, w: (0, 0, 0, 0))],
        out_specs=qo, scratch_shapes=[pltpu.VMEM((B_HEADS, nq, 2 * nq), F32)])
    return pl.pallas_call(
        functools.partial(_na_kernel, rows=rows), grid_spec=grid_spec,
        out_shape=jax.ShapeDtypeStruct((t, B_W), BF16),
        compiler_params=_cparams("arbitrary", "arbitrary"), name="neighbourhood_attention",
    )(which, q, k, v, pair_tiles)


def _dot_select(x, sel):
    hi = x.astype(BF16)
    r1 = x - hi.astype(F32)
    mid = r1.astype(BF16)
    lo = (r1 - mid.astype(F32)).astype(BF16)
    return _dot(hi, sel) + _dot(mid, sel) + _dot(lo, sel)


def _conv_kernel(c_ref, prev_ref, next_ref, w_ref, seg_ref, sm_ref, e_ref, cum_f_ref, cum_b_ref,
                 q_ref, k_ref, kbf_ref, kbb_ref, vbf_ref, vbb_ref, gcf_ref, gcb_ref, pad_ref, *, tiles_per_seq):
    i = pl.program_id(0)
    tm = c_ref.shape[0]
    halo = prev_ref.shape[0]
    first = (i % tiles_per_seq) == 0
    last = (i % tiles_per_seq) == tiles_per_seq - 1
    pad_ref[0:halo, :] = jnp.where(first, 0.0, prev_ref[...])
    pad_ref[halo:halo + tm, :] = c_ref[...]
    pad_ref[halo + tm:2 * halo + tm, :] = jnp.where(last, 0.0, next_ref[...])
    acc = None
    for j in range(CONV_K):
        o = halo - CONV_K // 2 + j
        term = w_ref[j:j + 1, :] * pad_ref[o:o + tm, :]
        acc = term if acc is None else acc + term
    y = _silu(acc)
    seg = seg_ref[...]

    def l2n(u):
        return u * lax.rsqrt(_seg_sum(u * u, seg) + 1e-6)

    q_ref[...] = (l2n(y[:, 0:C_W]) * (HEAD_DIM ** -0.5)).astype(BF16)
    k = l2n(y[:, C_W:2 * C_W])
    v = y[:, 2 * C_W:3 * C_W]
    k_ref[...] = k.astype(BF16)

    sm = sm_ref[...]
    beta_f = _dot_select(sm, e_ref[:, 0 * C_W:1 * C_W])
    beta_b = _dot_select(sm, e_ref[:, 1 * C_W:2 * C_W])
    kbf_ref[...] = (k * beta_f).astype(BF16)
    kbb_ref[...] = (k * beta_b).astype(BF16)
    vbf_ref[...] = (v * beta_f).astype(BF16)
    vbb_ref[...] = (v * beta_b).astype(BF16)
    gcf_ref[...] = _dot_select_lhs(cum_f_ref[...], _dot_select(sm, e_ref[:, 2 * C_W:3 * C_W]))
    gcb_ref[...] = _dot_select_lhs(cum_b_ref[...], _dot_select(sm, e_ref[:, 3 * C_W:4 * C_W]))


def _dot_select_lhs(sel, x):
    hi = x.astype(BF16)
    r1 = x - hi.astype(F32)
    mid = r1.astype(BF16)
    lo = (r1 - mid.astype(F32)).astype(BF16)
    return _dot(sel, hi) + _dot(sel, mid) + _dot(sel, lo)


def _short_conv_norm(c, sm, conv_w, seg, seq):
    t, w = c.shape
    tm = TOKEN_TILE
    halo = 8
    nb = tm // halo
    last_blk = t // halo - 1
    e, cum_f, cum_b = _gdn_constants(tm)
    out = pl.BlockSpec((tm, C_W), lambda i: (i, 0))

    def const(a):
        return pl.BlockSpec(a.shape, lambda i: (0, 0))

    return pl.pallas_call(
        functools.partial(_conv_kernel, tiles_per_seq=seq // tm), grid=(t // tm,),
        in_specs=[pl.BlockSpec((tm, w), lambda i: (i, 0)),
                  pl.BlockSpec((halo, w), lambda i: (jnp.maximum(i * nb - 1, 0), 0)),
                  pl.BlockSpec((halo, w), lambda i: (jnp.minimum((i + 1) * nb, last_blk), 0)),
                  pl.BlockSpec((8, w), lambda i: (0, 0)),
                  pl.BlockSpec((C_W, C_W), lambda i: (0, 0)),
                  pl.BlockSpec((tm, LANES), lambda i: (i, 0)), const(e), const(cum_f), const(cum_b)],
        out_specs=[out] * 8,
        out_shape=[jax.ShapeDtypeStruct((t, C_W), BF16)] * 6 + [jax.ShapeDtypeStruct((t, C_W), F32)] * 2,
        scratch_shapes=[pltpu.VMEM((tm + 2 * halo, w), F32)],
        compiler_params=_cparams("parallel"), name="gdn_conv_norm",
    )(c, c, c, conv_w, seg, sm, e, cum_f, cum_b)


def _block_diag(x, same_head):
    return jnp.where(same_head, jnp.concatenate([x] * C_HEADS, axis=0), 0.0)


def _gdn_prepare(chains):
    C = GDN_CHUNK
    ri = lax.broadcasted_iota(jnp.int32, (C, C_W), 0)
    lj = lax.broadcasted_iota(jnp.int32, (C, C_W), 1) % C
    rr = lax.broadcasted_iota(jnp.int32, (C_HEADS * C, C_W), 0) // C
    ll = lax.broadcasted_iota(jnp.int32, (C_HEADS * C, C_W), 1) // HEAD_DIM
    same_head = rr == ll
    eye = jnp.where(ri == lj, 1.0, 0.0)

    def bd(x):
        return _block_diag(x.astype(BF16), same_head)

    n = range(len(chains))
    decay = [jnp.exp(jnp.where((ri <= lj) if ch["reverse"] else (ri >= lj), ch["gc"] - ch["grow"], -jnp.inf))
             for ch in chains]
    r = [_dot_nt(jnp.concatenate([ch["kb"], ch["q"]], axis=0), bd(ch["k"])) for ch in chains]
    m = [jnp.where((ri < lj) if chains[i]["reverse"] else (ri > lj), r[i][:C] * decay[i], 0.0) for i in n]
    qk = [(r[i][C:] * decay[i]).astype(BF16) for i in n]

    p = [eye - m[i] for i in n]
    a = [_dot(m[i].astype(BF16), bd(m[i])) for i in n]
    for _ in range(int(math.log2(C)) - 2):
        r2 = [_dot(jnp.concatenate([a[i], p[i]], axis=0).astype(BF16), bd(a[i])) for i in n]
        a = [r2[i][:C] for i in n]
        p = [p[i] + r2[i][C:] for i in n]
    t_inv = [(p[i] + _dot(p[i].astype(BF16), bd(a[i]))).astype(BF16) for i in n]

    egc = [jnp.exp(ch["gc"]) for ch in chains]
    uw = [_dot(t_inv[i], jnp.concatenate([bd(chains[i]["vb"]), bd(chains[i]["kb"].astype(F32) * egc[i])], axis=1))
          for i in n]
    out = []
    for i in n:
        ch = chains[i]
        gc = ch["gc"]
        g_last = gc[0:1, :] if ch["reverse"] else gc[C - 1:C, :]
        ke = (ch["k"].astype(F32) * jnp.exp(g_last - gc)).astype(BF16)
        ab = _dot_tn(ke, uw[i].astype(BF16))
        u, w = uw[i][:, :C_W], uw[i][:, C_W:]
        qo = _dot(qk[i], jnp.concatenate([bd(u), bd(w)], axis=1))
        out.append(dict(
            a=jnp.where(same_head, -ab[:, C_W:], 0.0).astype(BF16),
            b=jnp.where(same_head, ab[:, :C_W], 0.0),
            d=jnp.exp(g_last),
            qeff=(ch["q"].astype(F32) * egc[i] - qo[:, C_W:]).astype(BF16),
            oloc=qo[:, :C_W]))
    return out


GDN_PREP_CHUNKS = 2


def _gdn_kernel(qf, kf, kbf, vbf, gcf, gwf, qb, kb, kbb, vbb, gcb, gwb, of_ref, ob_ref,
                state_ref, a_ref, b_ref, d_ref, qeff_ref, oloc_ref):
    C = GDN_CHUNK
    n = GDN_BLOCK // C
    dirs = ((qf, kf, kbf, vbf, gcf, gwf, False), (qb, kb, kbb, vbb, gcb, gwb, True))

    @pl.when(pl.program_id(1) == 0)
    def _():
        state_ref[...] = jnp.zeros_like(state_ref)

    def prepare(it, carry):
        chains, where = [], []
        for d, (q, k, kbeta, vbeta, gc, gw, reverse) in enumerate(dirs):
            for j in range(GDN_PREP_CHUNKS):
                c = it * GDN_PREP_CHUNKS + j
                sl = pl.ds(pl.multiple_of(c * C, C), C)
                chains.append(dict(q=q[sl, :], k=k[sl, :], kb=kbeta[sl, :], vb=vbeta[sl, :], gc=gc[sl, :],
                                   grow=gw[0, pl.ds(c, 1), :], reverse=reverse))
                where.append((d, c))
        for (d, c), res in zip(where, _gdn_prepare(chains)):
            a_ref[d, c] = res["a"]
            b_ref[d, c] = res["b"]
            d_ref[d, c] = res["d"]
            qeff_ref[d, c] = res["qeff"]
            oloc_ref[d, c] = res["oloc"]
        return carry

    lax.fori_loop(0, n // GDN_PREP_CHUNKS, prepare, 0)

    def scan(step, carry):
        cs = (step, n - 1 - step)
        states = [state_ref[d] for d in range(2)]
        lhs = [jnp.concatenate([a_ref[d, cs[d]], qeff_ref[d, cs[d]]], axis=0) for d in range(2)]
        res = [_dot(lhs[d], states[d].astype(BF16)) for d in range(2)]
        for d, o_ref in enumerate((of_ref, ob_ref)):
            state_ref[d] = states[d] * d_ref[d, cs[d]] + res[d][:C_W] + b_ref[d, cs[d]]
            o_ref[pl.ds(pl.multiple_of(cs[d] * C, C), C), :] = res[d][C_W:] + oloc_ref[d, cs[d]]
        return carry

    lax.fori_loop(0, n, scan, 0)


def _gdn_constants(tile):
    C = GDN_CHUNK
    e = np.zeros((LANES, 4 * C_W), np.float32)
    for kind in range(4):
        for h in range(C_HEADS):
            e[kind * C_HEADS + h, kind * C_W + h * HEAD_DIM:kind * C_W + (h + 1) * HEAD_DIM] = 1.0
    i = np.arange(C)
    tril = (i[None, :] <= i[:, None]).astype(np.float32)
    cum_f = np.kron(np.eye(tile // C, dtype=np.float32), tril)
    cum_b = np.kron(np.eye(tile // C, dtype=np.float32), tril.T)
    return tuple(jnp.asarray(a, BF16) for a in (e, cum_f, cum_b))


def _gated_delta(q, k, kb_f, kb_b, vb_f, vb_b, gc_f, gc_b, batch, seq):
    t = q.shape[0]
    nb = seq // GDN_BLOCK
    nc = GDN_BLOCK // GDN_CHUNK

    def rows(gc):
        g = gc[:, ::HEAD_DIM].reshape(t // GDN_CHUNK, GDN_CHUNK, C_HEADS)
        return jnp.transpose(g, (0, 2, 1)).reshape(t // GDN_BLOCK, nc, C_W)

    def fwd(w):
        return pl.BlockSpec((GDN_BLOCK, w), lambda b, s: (b * nb + s, 0))

    def bwd(w):
        return pl.BlockSpec((GDN_BLOCK, w), lambda b, s: (b * nb + nb - 1 - s, 0))

    gf_spec = pl.BlockSpec((1, nc, C_W), lambda b, s: (b * nb + s, 0, 0))
    gb_spec = pl.BlockSpec((1, nc, C_W), lambda b, s: (b * nb + nb - 1 - s, 0, 0))
    return pl.pallas_call(
        _gdn_kernel, grid=(batch, nb),
        in_specs=[fwd(C_W)] * 5 + [gf_spec] + [bwd(C_W)] * 5 + [gb_spec],
        out_specs=[fwd(C_W), bwd(C_W)],
        out_shape=[jax.ShapeDtypeStruct((t, C_W), F32)] * 2,
        scratch_shapes=[pltpu.VMEM((2, C_W, C_W), F32),
                        pltpu.VMEM((2, nc, C_W, C_W), BF16), pltpu.VMEM((2, nc, C_W, C_W), F32),
                        pltpu.VMEM((2, nc, 1, C_W), F32),
                        pltpu.VMEM((2, nc, GDN_CHUNK, C_W), BF16), pltpu.VMEM((2, nc, GDN_CHUNK, C_W), F32)],
        compiler_params=_cparams("parallel", "arbitrary"), name="gated_delta_rule",
    )(q, k, kb_f, vb_f, gc_f, rows(gc_f), q, k, kb_b, vb_b, gc_b, rows(gc_b))


def _merge_kernel(h_ref, ya_ref, yb_ref, of_ref, ob_ref, z_ref, wg_ref, wa_ref, wb_ref, wc_ref, wo_ref,
                  gn_ref, seg_ref, lg_ref, lb_ref, wrh_ref, wrl_ref, rb_ref, tril_ref,
                  h1_ref, h1b_ref, route_ref, count_ref):
    h = h_ref[...]
    x = h.astype(BF16)
    tm = h.shape[0]
    d = D_MODEL
    o = of_ref[...] + ob_ref[...]
    ms = _seg_sum(o * o, seg_ref[...]) * (1.0 / HEAD_DIM)
    yc = o * lax.rsqrt(ms + 1e-6) * gn_ref[...] * _silu(z_ref[...].astype(F32))
    mix = _sigmoid(_dot(x, wg_ref[0, :, 0:d])) * _dot(ya_ref[...], wa_ref[0])
    mix = mix + _sigmoid(_dot(x, wg_ref[0, :, d:2 * d])) * _dot(yb_ref[...], wb_ref[0])
    mix = mix + _sigmoid(_dot(x, wg_ref[0, :, 2 * d:3 * d])) * _dot(yc.astype(BF16), wc_ref[0])
    y = DN_ALPHA * h + _dot(mix.astype(BF16), wo_ref[0])
    h1 = _layernorm_rows(y, lg_ref[...], lb_ref[...])
    h1_ref[...] = h1
    h1b_ref[...] = h1.astype(BF16)

    hi = h1.astype(BF16)
    lo = (h1 - hi.astype(F32)).astype(BF16)
    logits = _dot(hi, wrh_ref[...]) + _dot(lo, wrh_ref[...]) + _dot(hi, wrl_ref[...])
    scores = _sigmoid(logits)
    sel = scores + rb_ref[...]
    lane = lax.broadcasted_iota(jnp.int32, (tm, LANES), 1)
    in_grp = lane < N_GROUPS
    sel_m = [sel if m == 0 else pltpu.roll(sel, LANES - m * N_GROUPS, 1) for m in range(GROUP_SIZE)]
    sc_m = [scores if m == 0 else pltpu.roll(scores, LANES - m * N_GROUPS, 1) for m in range(GROUP_SIZE)]
    hi01, lo01 = jnp.maximum(sel_m[0], sel_m[1]), jnp.minimum(sel_m[0], sel_m[1])
    hi23, lo23 = jnp.maximum(sel_m[2], sel_m[3]), jnp.minimum(sel_m[2], sel_m[3])
    top1 = jnp.maximum(hi01, hi23)
    top2 = jnp.maximum(jnp.minimum(hi01, hi23), jnp.maximum(lo01, lo23))
    grp_score = jnp.where(in_grp, top1 + top2, -jnp.inf)
    best_score = jnp.max(grp_score, axis=-1, keepdims=True)
    best = jnp.min(jnp.where(grp_score == best_score, lane, LANES), axis=-1, keepdims=True)
    pick = lane == best
    val = [jnp.sum(jnp.where(pick, s, 0.0), axis=-1, keepdims=True) for s in sel_m]
    aff = [jnp.sum(jnp.where(pick, s, 0.0), axis=-1, keepdims=True) for s in sc_m]
    e_sel = [jnp.zeros((tm, 1), F32)] * TOP_K
    w_sel = [jnp.zeros((tm, 1), F32)] * TOP_K
    for m in range(GROUP_SIZE):
        rank = jnp.zeros((tm, 1), jnp.int32)
        for j in range(GROUP_SIZE):
            if j != m:
                ahead = (val[j] > val[m]) | ((val[j] == val[m]) & (j < m))
                rank = rank + ahead.astype(jnp.int32)
        for kk in range(TOP_K):
            e_sel[kk] = jnp.where(rank == kk, (best * GROUP_SIZE + m).astype(F32), e_sel[kk])
            w_sel[kk] = jnp.where(rank == kk, aff[m], w_sel[kk])
    tot = w_sel[0] + w_sel[1]

    lane_f = lane.astype(F32)
    chosen = [lane_f == e_sel[kk] for kk in range(TOP_K)]
    onehot = jnp.where(chosen[0] | chosen[1], 1.0, 0.0).astype(BF16)
    running = _dot(tril_ref[...], onehot)
    rank = [jnp.sum(jnp.where(chosen[kk], running, 0.0), axis=-1, keepdims=True) - 1.0 for kk in range(TOP_K)]
    count_ref[0] = running[tm - 1:tm, :]
    route_ref[...] = jnp.where(lane == 0, e_sel[0], jnp.where(lane == 1, e_sel[1], jnp.where(
        lane == 2, w_sel[0] / tot, jnp.where(lane == 3, w_sel[1] / tot, jnp.where(
            lane == 4, rank[0], jnp.where(lane == 5, rank[1], 0.0))))))


def _merge(h, ya, yb, o_f, o_b, z, layer_weights, layer, gn, seg, lg, lb, wr_hi, wr_lo, rb):
    t, d = h.shape
    tm = TOKEN_TILE

    def rows(w):
        return pl.BlockSpec((tm, w), lambda i: (i, 0))

    def const(a):
        return pl.BlockSpec(a.shape, lambda i: (0, 0))

    i = np.arange(tm)
    tril = jnp.asarray((i[None, :] <= i[:, None]).astype(np.float32), BF16)
    consts = (gn, seg, lg, lb, wr_hi, wr_lo, rb, tril)
    return pl.pallas_call(
        _merge_kernel, grid=(t // tm,),
        in_specs=([rows(d), rows(A_Q_W), rows(B_W), rows(C_W), rows(C_W), rows(C_W)]
                  + [_layer_block(w, layer) for w in layer_weights] + [const(a) for a in consts]),
        out_specs=[rows(d), rows(d), rows(LANES), pl.BlockSpec((1, 1, LANES), lambda i: (i, 0, 0))],
        out_shape=[jax.ShapeDtypeStruct((t, d), F32), jax.ShapeDtypeStruct((t, d), BF16),
                   jax.ShapeDtypeStruct((t, LANES), F32), jax.ShapeDtypeStruct((t // tm, 1, LANES), F32)],
        compiler_params=_cparams("parallel"), name="merge_outproj_ln_router",
    )(h, ya, yb, o_f, o_b, z, *layer_weights, *consts)


def _expert_kernel(te_ref, nv_ref, x_ref, w1_ref, w3_ref, w2_ref, y_ref, w1b_ref, w3b_ref, w2b_ref):
    i = pl.program_id(0)

    @pl.when((i == 0) | (te_ref[i] != te_ref[jnp.maximum(i - 1, 0)]))
    def _():
        w1b_ref[...] = w1_ref[0, 0].astype(BF16)
        w3b_ref[...] = w3_ref[0, 0].astype(BF16)
        w2b_ref[...] = w2_ref[0, 0].astype(BF16)

    @pl.when(i < nv_ref[0])
    def _():
        x = x_ref[...]
        hid = _silu(_dot(x, w1b_ref[...])) * _dot(x, w3b_ref[...])
        y_ref[...] = _dot(hid.astype(BF16), w2b_ref[...]).astype(BF16)

    @pl.when(i >= nv_ref[0])
    def _():
        y_ref[...] = jnp.zeros_like(y_ref)


def _expert_mlp(xs, tile_expert, n_valid, w1, w3, w2, layer):
    mp, d = xs.shape
    te = EXPERT_TILE
    grid_spec = pltpu.PrefetchScalarGridSpec(
        num_scalar_prefetch=2, grid=(mp // te,),
        in_specs=[pl.BlockSpec((te, d), lambda i, e, n: (i, 0)),
                  pl.BlockSpec((1, 1, d, D_EXPERT), lambda i, e, n: (layer, e[i], 0, 0)),
                  pl.BlockSpec((1, 1, d, D_EXPERT), lambda i, e, n: (layer, e[i], 0, 0)),
                  pl.BlockSpec((1, 1, D_EXPERT, d), lambda i, e, n: (layer, e[i], 0, 0))],
        out_specs=pl.BlockSpec((te, d), lambda i, e, n: (i, 0)),
        scratch_shapes=[pltpu.VMEM((d, D_EXPERT), BF16), pltpu.VMEM((d, D_EXPERT), BF16),
                        pltpu.VMEM((D_EXPERT, d), BF16)])
    return pl.pallas_call(
        _expert_kernel, grid_spec=grid_spec, out_shape=jax.ShapeDtypeStruct((mp, d), BF16),
        compiler_params=_cparams("arbitrary"), name="expert_mlp",
    )(tile_expert, n_valid, xs, w1, w3, w2)


def _dispatch_plan(route, tile_counts, t):
    te = EXPERT_TILE
    e = route[:, 0:TOP_K].astype(jnp.int32)
    rank = route[:, 2 * TOP_K:3 * TOP_K].astype(jnp.int32)
    tile_counts = tile_counts[:, 0, :N_EXPERTS].astype(jnp.int32)
    before = jnp.cumsum(tile_counts, axis=0) - tile_counts
    counts = tile_counts.sum(axis=0)
    padded = ((counts + te - 1) // te) * te
    ends = jnp.cumsum(padded)
    base = (ends - padded)[None, :] + before
    tile_of_token = jnp.arange(t, dtype=jnp.int32)[:, None] // (t // tile_counts.shape[0])
    pos = jnp.take(base.reshape(-1), tile_of_token * N_EXPERTS + e) + rank
    n_tiles = (t * TOP_K) // te + N_EXPERTS
    tile_start = jnp.arange(n_tiles, dtype=jnp.int32) * te
    tile_expert = jnp.minimum((tile_start[:, None] >= ends[None, :]).astype(jnp.int32).sum(axis=1), N_EXPERTS - 1)
    n_valid = (ends[-1] // te).astype(jnp.int32).reshape(1)
    src = jnp.zeros((n_tiles * te,), jnp.int32).at[pos.reshape(-1)].set(
        jnp.repeat(jnp.arange(t, dtype=jnp.int32), TOP_K))
    return pos, src, tile_expert, n_valid


def _combine_kernel(h_ref, y0_ref, y1_ref, route_ref, g_ref, b_ref, o_ref):
    lane = lax.broadcasted_iota(jnp.int32, route_ref.shape, 1)
    r = route_ref[...]
    w0 = jnp.sum(jnp.where(lane == 2, r, 0.0), axis=-1, keepdims=True)
    w1 = jnp.sum(jnp.where(lane == 3, r, 0.0), axis=-1, keepdims=True)
    y = w0 * y0_ref[...].astype(F32) + w1 * y1_ref[...].astype(F32)
    o_ref[...] = _layernorm_rows(DN_ALPHA * h_ref[...] + y, g_ref[...], b_ref[...])


def _combine(h1, y0, y1, route, g, b):
    t, d = h1.shape
    row = pl.BlockSpec((TOKEN_TILE, d), lambda i: (i, 0))
    vec = pl.BlockSpec((1, d), lambda i: (0, 0))
    return pl.pallas_call(
        _combine_kernel, grid=(t // TOKEN_TILE,),
        in_specs=[row, row, row, pl.BlockSpec((TOKEN_TILE, LANES), lambda i: (i, 0)), vec, vec],
        out_specs=row, out_shape=jax.ShapeDtypeStruct((t, d), F32),
        compiler_params=_cparams("parallel"), name="combine_ln",
    )(h1, y0, y1, route, g.reshape(1, d), b.reshape(1, d))


def _rope_tables(seq):
    tkn = np.arange(seq)
    pos = np.stack([tkn // GRID_W, tkn % GRID_W], axis=1).astype(np.float32)
    half = HEAD_DIM // 2
    inv = ROPE_THETA ** (-np.arange(0, half, 2, dtype=np.float32) / half)
    d = np.arange(LANES) % HEAD_DIM
    axis = d // half
    r = d % half
    ang = pos[:, axis] * inv[r % (half // 2)][None, :]
    sign = np.where(r < half // 2, -1.0, 1.0)[None, :]
    return jnp.asarray(np.cos(ang), F32), jnp.asarray(np.sin(ang) * sign, F32)


def _segment_matrix(width):
    i = np.arange(width) // HEAD_DIM
    return jnp.asarray((i[:, None] == i[None, :]).astype(np.float32), BF16)


def _router_layout(w_router, router_bias):
    perm = np.array([(l % N_GROUPS) * GROUP_SIZE + l // N_GROUPS for l in range(N_EXPERTS)])
    w = jnp.pad(w_router[:, perm], ((0, 0), (0, LANES - N_EXPERTS)))
    hi = w.astype(BF16)
    lo = (w - hi.astype(F32)).astype(BF16)
    rb = jnp.pad(router_bias[perm], (0, LANES - N_EXPERTS)).reshape(1, LANES)
    return hi, lo, rb


def kernel(x, ln0_g, ln0_b, w_in, q_norm_g, k_norm_g, na_rpb, conv_w, A_log, dt_bias, gdn_norm_g,
           w_branch_a, w_branch_b, w_branch_c, w_out, ln1_g, ln1_b, w_router, router_bias, w1, w3, w2,
           ln2_g, ln2_b):
    batch, seq, d = x.shape
    t = batch * seq
    depth = w_in.shape[0]
    rows = seq // GRID_W
    cos, sin = _rope_tables(seq)
    seg128 = _segment_matrix(LANES)
    seg256 = _segment_matrix(C_W)
    wr_hi, wr_lo, rb = _router_layout(w_router, router_bias)

    w_main = w_in[:, :, :MAIN_W].astype(BF16)
    w_small = jnp.pad(w_in[:, :, MAIN_W:MAIN_W + SMALL_W], ((0, 0), (0, 0), (0, LANES - SMALL_W))).astype(BF16)
    merge_weights = (w_in[:, :, MAIN_W + SMALL_W:].astype(BF16), w_branch_a.astype(BF16),
                     w_branch_b.astype(BF16), w_branch_c.astype(BF16), w_out.astype(BF16))

    h = _input_layernorm(x.reshape(t, d), ln0_g, ln0_b)
    for l in range(depth):
        gp = jnp.zeros((8, LANES), F32)
        gp = gp.at[0, 2 * C_HEADS:4 * C_HEADS].set(A_log[l].reshape(-1))
        gp = gp.at[1, 2 * C_HEADS:4 * C_HEADS].set(dt_bias[l].reshape(-1))
        qg = jnp.tile(q_norm_g[l], LANES // HEAD_DIM).reshape(1, LANES)
        kg = jnp.tile(k_norm_g[l], LANES // HEAD_DIM).reshape(1, LANES)
        qpad, ak, av, bq, bk, bv, cqkv, cz, sm = _projection(h, w_main, w_small, l, cos, sin, qg, kg, seg128, gp, seq)

        ya = _gqa(qpad, ak, av, batch, seq)
        yb = _neighbourhood(bq, bk, bv, *_na_bias_tiles(na_rpb[l], rows), batch, seq)
        conv_pad = jnp.pad(conv_w[l], ((0, 8 - CONV_K), (0, 0)))
        cq, ck, kb_f, kb_b, vb_f, vb_b, gc_f, gc_b = _short_conv_norm(cqkv, sm, conv_pad, seg256, seq)
        o_f, o_b = _gated_delta(cq, ck, kb_f, kb_b, vb_f, vb_b, gc_f, gc_b, batch, seq)

        gn = jnp.tile(gdn_norm_g[l], C_HEADS).reshape(1, C_W)
        h1, h1_bf16, route, tile_counts = _merge(h, ya, yb, o_f, o_b, cz, merge_weights, l, gn, seg256,
                                                 ln1_g[l].reshape(1, d), ln1_b[l].reshape(1, d), wr_hi, wr_lo, rb)

        pos, src, tile_expert, n_valid = _dispatch_plan(route, tile_counts, t)
        xs = jnp.take(h1_bf16, src, axis=0)
        ys = _expert_mlp(xs, tile_expert, n_valid, w1, w3, w2, l)
        y0 = jnp.take(ys, pos[:, 0], axis=0)
        y1 = jnp.take(ys, pos[:, 1], axis=0)
        h = _combine(h1, y0, y1, route, ln2_g[l], ln2_b[l])
    return h.reshape(batch, seq, d)
```

```python
import functools
import math

import numpy as np
import jax
import jax.numpy as jnp
from jax import lax
from jax.experimental import pallas as pl
from jax.experimental.pallas import tpu as pltpu

F32 = jnp.float32
BF16 = jnp.bfloat16

D_MODEL = 1024
GRID_W = 64
HEAD_DIM = 64
A_Q_HEADS = 8
A_KV_HEADS = 2
ROPE_THETA = 10000.0
B_HEADS = 4
WIN_R = 8
WIN_C = 16
C_HEADS = 4
CONV_K = 5
N_EXPERTS = 32
N_GROUPS = 8
GROUP_SIZE = N_EXPERTS // N_GROUPS
TOP_K = 2
D_EXPERT = 512
DEPTH = 4
DN_ALPHA = (2 * DEPTH) ** 0.25

A_Q_W = A_Q_HEADS * HEAD_DIM
A_KV_W = A_KV_HEADS * HEAD_DIM
B_W = B_HEADS * HEAD_DIM
C_W = C_HEADS * HEAD_DIM
MAIN_W = A_Q_W + 2 * A_KV_W + 3 * B_W + 4 * C_W
SMALL_W = 4 * C_HEADS

LANES = 128
TOKEN_TILE = 512
ATTN_Q_TILE = 512
NA_ROWS_PER_BLOCK = 8
GDN_CHUNK = 64
GDN_BLOCK = 512
EXPERT_TILE = 256
SEQUENCE_GROUPS = 2
VMEM_LIMIT = 48 * 1024 * 1024
NEG_BIG = -1e30
LOG2_E = math.log2(math.e)
HIGHEST = lax.Precision.HIGHEST


def _cparams(*sem):
    return pltpu.CompilerParams(dimension_semantics=sem, vmem_limit_bytes=VMEM_LIMIT)


def _dot(a, b):
    return jnp.dot(a, b, preferred_element_type=F32)


def _dot_nt(a, b):
    return lax.dot_general(a, b, (((1,), (1,)), ((), ())), preferred_element_type=F32)


def _dot_tn(a, b):
    return lax.dot_general(a, b, (((0,), (0,)), ((), ())), preferred_element_type=F32)


def _dot_exact(a, b):
    return jnp.dot(a, b, preferred_element_type=F32, precision=HIGHEST)


def _seg_sum(y, seg):
    hi = y.astype(BF16)
    lo = (y - hi.astype(F32)).astype(BF16)
    return _dot(hi, seg) + _dot(lo, seg)


def _sigmoid(x):
    return 1.0 / (1.0 + jnp.exp(-x))


def _silu(x):
    return x * _sigmoid(x)


def _layernorm_rows(y, g, b):
    mu = jnp.mean(y, axis=-1, keepdims=True)
    yc = y - mu
    var = jnp.mean(yc * yc, axis=-1, keepdims=True)
    return yc * lax.rsqrt(var + 1e-5) * g + b


def _ln_kernel(x_ref, g_ref, b_ref, o_ref):
    o_ref[...] = _layernorm_rows(x_ref[...], g_ref[...], b_ref[...])


def _input_layernorm(x, g, b):
    t, d = x.shape
    row = pl.BlockSpec((TOKEN_TILE, d), lambda i: (i, 0))
    vec = pl.BlockSpec((1, d), lambda i: (0, 0))
    return pl.pallas_call(
        _ln_kernel, grid=(t // TOKEN_TILE,), in_specs=[row, vec, vec], out_specs=row,
        out_shape=jax.ShapeDtypeStruct((t, d), F32), compiler_params=_cparams("parallel"),
        name="input_layernorm")(x, g.reshape(1, d), b.reshape(1, d))


def _proj_kernel(h_ref, w_ref, ws_ref, cos_ref, sin_ref, qg_ref, kg_ref, seg_ref, gp_ref,
                 q_ref, k_ref, v_ref, bq_ref, bk_ref, bv_ref, c_ref, z_ref, sm_ref):
    x = h_ref[...].astype(BF16)
    tm = x.shape[0]
    lane = lax.broadcasted_iota(jnp.int32, (tm, LANES), 1)
    first_half = (lane % (HEAD_DIM // 2)) < (HEAD_DIM // 4)
    low_head = lane < HEAD_DIM
    cos = cos_ref[...]
    sin = sin_ref[...]
    seg = seg_ref[...]

    def mm(a, b):
        return _dot(x, w_ref[0, :, a:b])

    def norm_rope(y, g):
        ms = _seg_sum(y * y, seg) * (1.0 / HEAD_DIM)
        yn = y * lax.rsqrt(ms + 1e-6) * g
        partner = jnp.where(first_half, pltpu.roll(yn, LANES - HEAD_DIM // 4, 1),
                            pltpu.roll(yn, HEAD_DIM // 4, 1))
        return yn * cos + partner * sin

    for j in range(A_Q_W // LANES):
        y = norm_rope(mm(j * LANES, (j + 1) * LANES), qg_ref[...]) * (LOG2_E * HEAD_DIM ** -0.5)
        yr = pltpu.roll(y, HEAD_DIM, 1)
        kv_head = (2 * j) // (A_Q_HEADS // A_KV_HEADS)
        if kv_head == 0:
            even, odd = jnp.where(low_head, y, 0.0), jnp.where(low_head, yr, 0.0)
        else:
            even, odd = jnp.where(low_head, 0.0, yr), jnp.where(low_head, 0.0, y)
        q_ref[:, (2 * j) * LANES:(2 * j + 1) * LANES] = even.astype(BF16)
        q_ref[:, (2 * j + 1) * LANES:(2 * j + 2) * LANES] = odd.astype(BF16)
    o = A_Q_W
    k_ref[...] = norm_rope(mm(o, o + A_KV_W), kg_ref[...]).astype(BF16)
    o += A_KV_W
    v_ref[:, 0:A_KV_W] = mm(o, o + A_KV_W).astype(BF16)
    v_ref[:, A_KV_W:2 * A_KV_W] = jnp.ones((tm, A_KV_W), BF16)
    o += A_KV_W
    bq_ref[...] = (mm(o, o + B_W) * (HEAD_DIM ** -0.5)).astype(BF16)
    o += B_W
    bk_ref[...] = mm(o, o + B_W).astype(BF16)
    o += B_W
    bv_ref[...] = mm(o, o + B_W).astype(BF16)
    o += B_W
    c_ref[...] = mm(o, o + 3 * C_W)
    o += 3 * C_W
    z_ref[...] = mm(o, o + C_W).astype(BF16)

    s = _dot(x, ws_ref[0])
    a = jnp.exp(gp_ref[0:1, :])
    u = s + gp_ref[1:2, :]
    softplus = jnp.maximum(u, 0.0) + jnp.log(1.0 + jnp.exp(-jnp.abs(u)))
    sm_ref[...] = jnp.where(lane < 2 * C_HEADS, _sigmoid(s), jnp.where(lane < SMALL_W, -a * softplus, 0.0))


def _layer_block(a, layer):
    return pl.BlockSpec((1,) + a.shape[1:], lambda i: (layer,) + (0,) * (a.ndim - 1))


def _projection(h, w_main, w_small, layer, cos, sin, qg, kg, seg, gp, seq):
    t, d = h.shape
    tm = TOKEN_TILE
    nseq = seq // tm

    def rows(w):
        return pl.BlockSpec((tm, w), lambda i: (i, 0))

    def const(shape):
        return pl.BlockSpec(shape, lambda i: (0, 0))

    pos = pl.BlockSpec((tm, LANES), lambda i: (i % nseq, 0))
    out_w = [(2 * A_Q_W, BF16), (A_KV_W, BF16), (2 * A_KV_W, BF16), (B_W, BF16), (B_W, BF16), (B_W, BF16),
             (3 * C_W, F32), (C_W, BF16), (LANES, F32)]
    return pl.pallas_call(
        _proj_kernel, grid=(t // tm,),
        in_specs=[rows(d), _layer_block(w_main, layer), _layer_block(w_small, layer), pos, pos, const((1, LANES)),
                  const((1, LANES)), const((LANES, LANES)), const((8, LANES))],
        out_specs=[rows(w) for w, _ in out_w],
        out_shape=[jax.ShapeDtypeStruct((t, w), dt) for w, dt in out_w],
        compiler_params=_cparams("parallel"), name="projection",
    )(h, w_main, w_small, cos, sin, qg, kg, seg, gp)


def _gqa_kernel(q_ref, k_ref, v1_ref, o_ref):
    tq = q_ref.shape[0]
    lane = lax.broadcasted_iota(jnp.int32, (tq, LANES), 1)
    low_head = lane < HEAD_DIM
    k = k_ref[...]
    v1 = v1_ref[...]
    group = A_Q_HEADS // A_KV_HEADS
    outs = []
    for h in range(A_Q_HEADS):
        s = _dot_nt(q_ref[:, h * LANES:(h + 1) * LANES], k)
        p = jnp.exp2(s - jnp.max(s, axis=-1, keepdims=True)).astype(BF16)
        acc = _dot(p, v1)
        outs.append(acc[:, 0:A_KV_W] / acc[:, A_KV_W:A_KV_W + 1])
    for j in range(A_Q_HEADS // 2):
        even, odd = outs[2 * j], outs[2 * j + 1]
        if (2 * j) // group == 0:
            blk = jnp.where(low_head, even, pltpu.roll(odd, HEAD_DIM, 1))
        else:
            blk = jnp.where(low_head, pltpu.roll(even, HEAD_DIM, 1), odd)
        o_ref[:, j * LANES:(j + 1) * LANES] = blk.astype(BF16)


def _gqa(qpad, k, v1, batch, seq):
    t = qpad.shape[0]
    tq = ATTN_Q_TILE
    nq = seq // tq
    return pl.pallas_call(
        _gqa_kernel, grid=(batch, nq),
        in_specs=[pl.BlockSpec((tq, 2 * A_Q_W), lambda b, i: (b * nq + i, 0)),
                  pl.BlockSpec((seq, A_KV_W), lambda b, i: (b, 0)),
                  pl.BlockSpec((seq, 2 * A_KV_W), lambda b, i: (b, 0))],
        out_specs=pl.BlockSpec((tq, A_Q_W), lambda b, i: (b * nq + i, 0)),
        out_shape=jax.ShapeDtypeStruct((t, A_Q_W), BF16),
        compiler_params=_cparams("parallel", "parallel"), name="gqa_attention",
    )(qpad, k, v1)


def _na_key_row0(qb, rows):
    return min(max(NA_ROWS_PER_BLOCK * qb - WIN_R // 2, 0), rows - 2 * NA_ROWS_PER_BLOCK)


def _na_kernel(which_ref, q_ref, k_ref, v_ref, pair_ref, o_ref, bias_ref, *, rows):
    qb = pl.program_id(0)
    nq = q_ref.shape[0]
    nk = 2 * NA_ROWS_PER_BLOCK * GRID_W

    @pl.when(pl.program_id(1) == 0)
    def _():
        n = NA_ROWS_PER_BLOCK
        for rq in range(n):
            for rp in range(n):
                tile = which_ref[(qb * n + rq) * n + rp]
                for h in range(B_HEADS):
                    bias_ref[h, rq * GRID_W:(rq + 1) * GRID_W, rp * LANES:(rp + 1) * LANES] = pair_ref[h, tile]

    row0 = jnp.clip(NA_ROWS_PER_BLOCK * qb - WIN_R // 2, 0, rows - 2 * NA_ROWS_PER_BLOCK)
    start = pl.multiple_of(row0 * GRID_W, 4 * GRID_W)
    lane = lax.broadcasted_iota(jnp.int32, (nq, LANES), 1)
    low_head = lane < HEAD_DIM
    zero = jnp.zeros((), BF16)
    for j in range(B_W // LANES):
        kblk = k_ref[pl.ds(start, nk), j * LANES:(j + 1) * LANES]
        vblk = v_ref[pl.ds(start, nk), j * LANES:(j + 1) * LANES]
        qblk = q_ref[:, j * LANES:(j + 1) * LANES]
        outs = []
        for par in range(2):
            qh = jnp.where(low_head if par == 0 else jnp.logical_not(low_head), qblk, zero)
            s = _dot_nt(qh, kblk) + bias_ref[2 * j + par]
            m = jnp.max(s, axis=-1, keepdims=True)
            p = jnp.exp(s - m)
            l = jnp.sum(p, axis=-1, keepdims=True)
            outs.append(_dot(p.astype(BF16), vblk) / l)
        o_ref[:, j * LANES:(j + 1) * LANES] = jnp.where(low_head, outs[0], outs[1]).astype(BF16)


def _na_bias_tiles(rpb, rows):
    nqb = rows // NA_ROWS_PER_BLOCK
    nkr = 2 * NA_ROWS_PER_BLOCK
    wr = min(WIN_R, rows)
    n_dr = 2 * WIN_R - 1
    c = np.arange(GRID_W)
    c0 = np.clip(c - WIN_C // 2, 0, GRID_W - WIN_C)
    in_c = (c[None, :] >= c0[:, None]) & (c[None, :] < c0[:, None] + WIN_C)
    dc = np.clip(c[None, :] - c[:, None] + (WIN_C - 1), 0, 2 * WIN_C - 2)
    col_tiles = jnp.where(jnp.asarray(in_c)[None, None], rpb[:, :, jnp.asarray(dc)], NEG_BIG)
    col_tiles = jnp.concatenate([col_tiles, jnp.full((B_HEADS, 1, GRID_W, GRID_W), NEG_BIG, F32)], axis=1)
    tile_idx = np.full((nqb, NA_ROWS_PER_BLOCK, nkr), n_dr, np.int32)
    for qb in range(nqb):
        for rq in range(NA_ROWS_PER_BLOCK):
            r = NA_ROWS_PER_BLOCK * qb + rq
            r0 = min(max(r - wr // 2, 0), rows - wr)
            for rk in range(nkr):
                kr = _na_key_row0(qb, rows) + rk
                if r0 <= kr < r0 + wr:
                    tile_idx[qb, rq, rk] = kr - r + (WIN_R - 1)
    pairs, which = np.unique(tile_idx.reshape(-1, 2), axis=0, return_inverse=True)
    pair_tiles = jnp.concatenate([jnp.take(col_tiles, jnp.asarray(pairs[:, 0]), axis=1),
                                  jnp.take(col_tiles, jnp.asarray(pairs[:, 1]), axis=1)], axis=-1)
    return pair_tiles, jnp.asarray(which.reshape(-1).astype(np.int32))


def _neighbourhood(q, k, v, pair_tiles, which, batch, seq):
    t = q.shape[0]
    rows = seq // GRID_W
    nqb = rows // NA_ROWS_PER_BLOCK
    nq = NA_ROWS_PER_BLOCK * GRID_W
    kv = pl.BlockSpec((seq, B_W), lambda qb, b, w: (b, 0))
    qo = pl.BlockSpec((nq, B_W), lambda qb, b, w: (b * nqb + qb, 0))
    grid_spec = pltpu.PrefetchScalarGridSpec(
        num_scalar_prefetch=1, grid=(nqb, batch),
        in_specs=[qo, kv, kv, pl.BlockSpec(pair_tiles.shape, lambda qb, b, w: (0, 0, 0, 0))],
        out_specs=qo, scratch_shapes=[pltpu.VMEM((B_HEADS, nq, 2 * nq), F32)])
    return pl.pallas_call(
        functools.partial(_na_kernel, rows=rows), grid_spec=grid_spec,
        out_shape=jax.ShapeDtypeStruct((t, B_W), BF16),
        compiler_params=_cparams("arbitrary", "arbitrary"), name="neighbourhood_attention",
    )(which, q, k, v, pair_tiles)


def _dot_select(x, sel):
    hi = x.astype(BF16)
    r1 = x - hi.astype(F32)
    mid = r1.astype(BF16)
    lo = (r1 - mid.astype(F32)).astype(BF16)
    return _dot(hi, sel) + _dot(mid, sel) + _dot(lo, sel)


def _conv_kernel(c_ref, prev_ref, next_ref, w_ref, seg_ref, sm_ref, e_ref, cum_f_ref, cum_b_ref,
                 q_ref, k_ref, kbf_ref, kbb_ref, vbf_ref, vbb_ref, gcf_ref, gcb_ref, gcs_ref, pad_ref, *,
                 tiles_per_seq):
    i = pl.program_id(0)
    tm = c_ref.shape[0]
    halo = prev_ref.shape[0]
    first = (i % tiles_per_seq) == 0
    last = (i % tiles_per_seq) == tiles_per_seq - 1
    pad_ref[0:halo, :] = jnp.where(first, 0.0, prev_ref[...])
    pad_ref[halo:halo + tm, :] = c_ref[...]
    pad_ref[halo + tm:2 * halo + tm, :] = jnp.where(last, 0.0, next_ref[...])
    acc = None
    for j in range(CONV_K):
        o = halo - CONV_K // 2 + j
        term = w_ref[j:j + 1, :] * pad_ref[o:o + tm, :]
        acc = term if acc is None else acc + term
    y = _silu(acc)
    seg = seg_ref[...]

    def l2n(u):
        return u * lax.rsqrt(_seg_sum(u * u, seg) + 1e-6)

    q_ref[...] = (l2n(y[:, 0:C_W]) * (HEAD_DIM ** -0.5)).astype(BF16)
    k = l2n(y[:, C_W:2 * C_W])
    v = y[:, 2 * C_W:3 * C_W]
    k_ref[...] = k.astype(BF16)

    sm = sm_ref[...]
    beta_f = _dot_select(sm, e_ref[:, 0 * C_W:1 * C_W])
    beta_b = _dot_select(sm, e_ref[:, 1 * C_W:2 * C_W])
    kbf_ref[...] = (k * beta_f).astype(BF16)
    kbb_ref[...] = (k * beta_b).astype(BF16)
    vbf_ref[...] = (v * beta_f).astype(BF16)
    vbb_ref[...] = (v * beta_b).astype(BF16)
    cum_f = _dot_select_lhs(cum_f_ref[...], sm)
    cum_b = _dot_select_lhs(cum_b_ref[...], sm)
    gcf_ref[...] = _dot_select(cum_f, e_ref[:, 2 * C_W:3 * C_W])
    gcb_ref[...] = _dot_select(cum_b, e_ref[:, 3 * C_W:4 * C_W])
    lane = lax.broadcasted_iota(jnp.int32, sm.shape, 1)
    gcs_ref[...] = jnp.where(lane < 3 * C_HEADS, cum_f, cum_b)


def _dot_select_lhs(sel, x):
    hi = x.astype(BF16)
    r1 = x - hi.astype(F32)
    mid = r1.astype(BF16)
    lo = (r1 - mid.astype(F32)).astype(BF16)
    return _dot(sel, hi) + _dot(sel, mid) + _dot(sel, lo)


def _short_conv_norm(c, sm, conv_w, seg, seq):
    t, w = c.shape
    tm = TOKEN_TILE
    halo = 8
    nb = tm // halo
    last_blk = t // halo - 1
    e, cum_f, cum_b = _gdn_constants(tm)
    out = pl.BlockSpec((tm, C_W), lambda i: (i, 0))

    def const(a):
        return pl.BlockSpec(a.shape, lambda i: (0, 0))

    return pl.pallas_call(
        functools.partial(_conv_kernel, tiles_per_seq=seq // tm), grid=(t // tm,),
        in_specs=[pl.BlockSpec((tm, w), lambda i: (i, 0)),
                  pl.BlockSpec((halo, w), lambda i: (jnp.maximum(i * nb - 1, 0), 0)),
                  pl.BlockSpec((halo, w), lambda i: (jnp.minimum((i + 1) * nb, last_blk), 0)),
                  pl.BlockSpec((8, w), lambda i: (0, 0)),
                  pl.BlockSpec((C_W, C_W), lambda i: (0, 0)),
                  pl.BlockSpec((tm, LANES), lambda i: (i, 0)), const(e), const(cum_f), const(cum_b)],
        out_specs=[out] * 8 + [pl.BlockSpec((tm, LANES), lambda i: (i, 0))],
        out_shape=([jax.ShapeDtypeStruct((t, C_W), BF16)] * 6 + [jax.ShapeDtypeStruct((t, C_W), F32)] * 2
                   + [jax.ShapeDtypeStruct((t, LANES), F32)]),
        scratch_shapes=[pltpu.VMEM((tm + 2 * halo, w), F32)],
        compiler_params=_cparams("parallel"), name="gdn_conv_norm",
    )(c, c, c, conv_w, seg, sm, e, cum_f, cum_b)


def _block_diag(x, same_head):
    return jnp.where(same_head, jnp.concatenate([x] * C_HEADS, axis=0), 0.0)


def _gdn_prepare(chains):
    C = GDN_CHUNK
    ri = lax.broadcasted_iota(jnp.int32, (C, C_W), 0)
    lj = lax.broadcasted_iota(jnp.int32, (C, C_W), 1) % C
    rr = lax.broadcasted_iota(jnp.int32, (C_HEADS * C, C_W), 0) // C
    ll = lax.broadcasted_iota(jnp.int32, (C_HEADS * C, C_W), 1) // HEAD_DIM
    same_head = rr == ll
    eye = jnp.where(ri == lj, 1.0, 0.0)

    def bd(x):
        return _block_diag(x.astype(BF16), same_head)

    n = range(len(chains))
    decay = [jnp.exp(jnp.where((ri <= lj) if ch["reverse"] else (ri >= lj), ch["gc"] - ch["grow"], -jnp.inf))
             for ch in chains]
    r = [_dot_nt(jnp.concatenate([ch["kb"], ch["q"]], axis=0), bd(ch["k"])) for ch in chains]
    m = [jnp.where((ri < lj) if chains[i]["reverse"] else (ri > lj), r[i][:C] * decay[i], 0.0) for i in n]
    qk = [(r[i][C:] * decay[i]).astype(BF16) for i in n]

    p = [eye - m[i] for i in n]
    a = [_dot(m[i].astype(BF16), bd(m[i])) for i in n]
    for _ in range(int(math.log2(C)) - 2):
        r2 = [_dot(jnp.concatenate([a[i], p[i]], axis=0).astype(BF16), bd(a[i])) for i in n]
        a = [r2[i][:C] for i in n]
        p = [p[i] + r2[i][C:] for i in n]
    t_inv = [(p[i] + _dot(p[i].astype(BF16), bd(a[i]))).astype(BF16) for i in n]

    egc = [jnp.exp(ch["gc"]) for ch in chains]
    uw = [_dot(t_inv[i], jnp.concatenate([bd(chains[i]["vb"]), bd(chains[i]["kb"].astype(F32) * egc[i])], axis=1))
          for i in n]
    out = []
    for i in n:
        ch = chains[i]
        gc = ch["gc"]
        g_last = gc[0:1, :] if ch["reverse"] else gc[C - 1:C, :]
        ke = (ch["k"].astype(F32) * jnp.exp(g_last - gc)).astype(BF16)
        ab = _dot_tn(ke, uw[i].astype(BF16))
        u, w = uw[i][:, :C_W], uw[i][:, C_W:]
        qo = _dot(qk[i], jnp.concatenate([bd(u), bd(w)], axis=1))
        out.append(dict(
            a=jnp.where(same_head, -ab[:, C_W:], 0.0).astype(BF16),
            b=jnp.where(same_head, ab[:, :C_W], 0.0),
            d=jnp.exp(g_last),
            qeff=(ch["q"].astype(F32) * egc[i] - qo[:, C_W:]).astype(BF16),
            oloc=qo[:, :C_W]))
    return out


GDN_PREP_CHUNKS = 2


def _gdn_kernel(qf, kf, kbf, vbf, gcf, gwf, qb, kb, kbb, vbb, gcb, gwb, of_ref, ob_ref,
                state_ref, a_ref, b_ref, d_ref, qeff_ref, oloc_ref):
    C = GDN_CHUNK
    n = GDN_BLOCK // C
    dirs = ((qf, kf, kbf, vbf, gcf, gwf, False), (qb, kb, kbb, vbb, gcb, gwb, True))

    @pl.when(pl.program_id(1) == 0)
    def _():
        state_ref[...] = jnp.zeros_like(state_ref)

    def prepare(it, carry):
        chains, where = [], []
        for d, (q, k, kbeta, vbeta, gc, gw, reverse) in enumerate(dirs):
            for j in range(GDN_PREP_CHUNKS):
                c = it * GDN_PREP_CHUNKS + j
                sl = pl.ds(pl.multiple_of(c * C, C), C)
                chains.append(dict(q=q[sl, :], k=k[sl, :], kb=kbeta[sl, :], vb=vbeta[sl, :], gc=gc[sl, :],
                                   grow=gw[0, pl.ds(c, 1), :], reverse=reverse))
                where.append((d, c))
        for (d, c), res in zip(where, _gdn_prepare(chains)):
            a_ref[d, c] = res["a"]
            b_ref[d, c] = res["b"]
            d_ref[d, c] = res["d"]
            qeff_ref[d, c] = res["qeff"]
            oloc_ref[d, c] = res["oloc"]
        return carry

    lax.fori_loop(0, n // GDN_PREP_CHUNKS, prepare, 0)

    def scan(step, carry):
        cs = (step, n - 1 - step)
        states = [state_ref[d] for d in range(2)]
        lhs = [jnp.concatenate([a_ref[d, cs[d]], qeff_ref[d, cs[d]]], axis=0) for d in range(2)]
        res = [_dot(lhs[d], states[d].astype(BF16)) for d in range(2)]
        for d, o_ref in enumerate((of_ref, ob_ref)):
            state_ref[d] = states[d] * d_ref[d, cs[d]] + res[d][:C_W] + b_ref[d, cs[d]]
            o_ref[pl.ds(pl.multiple_of(cs[d] * C, C), C), :] = res[d][C_W:] + oloc_ref[d, cs[d]]
        return carry

    lax.fori_loop(0, n, scan, 0)


def _gdn_constants(tile):
    C = GDN_CHUNK
    e = np.zeros((LANES, 4 * C_W), np.float32)
    for kind in range(4):
        for h in range(C_HEADS):
            e[kind * C_HEADS + h, kind * C_W + h * HEAD_DIM:kind * C_W + (h + 1) * HEAD_DIM] = 1.0
    i = np.arange(C)
    tril = (i[None, :] <= i[:, None]).astype(np.float32)
    cum_f = np.kron(np.eye(tile // C, dtype=np.float32), tril)
    cum_b = np.kron(np.eye(tile // C, dtype=np.float32), tril.T)
    return tuple(jnp.asarray(a, BF16) for a in (e, cum_f, cum_b))


def _gated_delta(q, k, kb_f, kb_b, vb_f, vb_b, gc_f, gc_b, gcs, batch, seq):
    t = q.shape[0]
    nb = seq // GDN_BLOCK
    nc = GDN_BLOCK // GDN_CHUNK

    def rows(col0):
        g = gcs[:, col0:col0 + C_HEADS].reshape(t // GDN_CHUNK, GDN_CHUNK, C_HEADS)
        return jnp.transpose(g, (0, 2, 1)).reshape(t // GDN_BLOCK, nc, C_W)

    def fwd(w):
        return pl.BlockSpec((GDN_BLOCK, w), lambda b, s: (b * nb + s, 0))

    def bwd(w):
        return pl.BlockSpec((GDN_BLOCK, w), lambda b, s: (b * nb + nb - 1 - s, 0))

    gf_spec = pl.BlockSpec((1, nc, C_W), lambda b, s: (b * nb + s, 0, 0))
    gb_spec = pl.BlockSpec((1, nc, C_W), lambda b, s: (b * nb + nb - 1 - s, 0, 0))
    return pl.pallas_call(
        _gdn_kernel, grid=(batch, nb),
        in_specs=[fwd(C_W)] * 5 + [gf_spec] + [bwd(C_W)] * 5 + [gb_spec],
        out_specs=[fwd(C_W), bwd(C_W)],
        out_shape=[jax.ShapeDtypeStruct((t, C_W), F32)] * 2,
        scratch_shapes=[pltpu.VMEM((2, C_W, C_W), F32),
                        pltpu.VMEM((2, nc, C_W, C_W), BF16), pltpu.VMEM((2, nc, C_W, C_W), F32),
                        pltpu.VMEM((2, nc, 1, C_W), F32),
                        pltpu.VMEM((2, nc, GDN_CHUNK, C_W), BF16), pltpu.VMEM((2, nc, GDN_CHUNK, C_W), F32)],
        compiler_params=_cparams("parallel", "arbitrary"), name="gated_delta_rule",
    )(q, k, kb_f, vb_f, gc_f, rows(2 * C_HEADS), q, k, kb_b, vb_b, gc_b, rows(3 * C_HEADS))


def _merge_kernel(h_ref, ya_ref, yb_ref, of_ref, ob_ref, z_ref, wg_ref, wa_ref, wb_ref, wc_ref, wo_ref,
                  gn_ref, seg_ref, lg_ref, lb_ref, wrh_ref, wrl_ref, rb_ref, tril_ref,
                  h1_ref, h1b_ref, route_ref, count_ref):
    h = h_ref[...]
    x = h.astype(BF16)
    tm = h.shape[0]
    d = D_MODEL
    o = of_ref[...] + ob_ref[...]
    ms = _seg_sum(o * o, seg_ref[...]) * (1.0 / HEAD_DIM)
    yc = o * lax.rsqrt(ms + 1e-6) * gn_ref[...] * _silu(z_ref[...].astype(F32))
    mix = _sigmoid(_dot(x, wg_ref[0, :, 0:d])) * _dot(ya_ref[...], wa_ref[0])
    mix = mix + _sigmoid(_dot(x, wg_ref[0, :, d:2 * d])) * _dot(yb_ref[...], wb_ref[0])
    mix = mix + _sigmoid(_dot(x, wg_ref[0, :, 2 * d:3 * d])) * _dot(yc.astype(BF16), wc_ref[0])
    y = DN_ALPHA * h + _dot(mix.astype(BF16), wo_ref[0])
    h1 = _layernorm_rows(y, lg_ref[...], lb_ref[...])
    h1_ref[...] = h1
    h1b_ref[...] = h1.astype(BF16)

    hi = h1.astype(BF16)
    lo = (h1 - hi.astype(F32)).astype(BF16)
    logits = _dot(hi, wrh_ref[...]) + _dot(lo, wrh_ref[...]) + _dot(hi, wrl_ref[...])
    scores = _sigmoid(logits)
    sel = scores + rb_ref[...]
    lane = lax.broadcasted_iota(jnp.int32, (tm, LANES), 1)
    in_grp = lane < N_GROUPS
    sel_m = [sel if m == 0 else pltpu.roll(sel, LANES - m * N_GROUPS, 1) for m in range(GROUP_SIZE)]
    sc_m = [scores if m == 0 else pltpu.roll(scores, LANES - m * N_GROUPS, 1) for m in range(GROUP_SIZE)]
    hi01, lo01 = jnp.maximum(sel_m[0], sel_m[1]), jnp.minimum(sel_m[0], sel_m[1])
    hi23, lo23 = jnp.maximum(sel_m[2], sel_m[3]), jnp.minimum(sel_m[2], sel_m[3])
    top1 = jnp.maximum(hi01, hi23)
    top2 = jnp.maximum(jnp.minimum(hi01, hi23), jnp.maximum(lo01, lo23))
    grp_score = jnp.where(in_grp, top1 + top2, -jnp.inf)
    best_score = jnp.max(grp_score, axis=-1, keepdims=True)
    best = jnp.min(jnp.where(grp_score == best_score, lane, LANES), axis=-1, keepdims=True)
    pick = lane == best
    val = [jnp.sum(jnp.where(pick, s, 0.0), axis=-1, keepdims=True) for s in sel_m]
    aff = [jnp.sum(jnp.where(pick, s, 0.0), axis=-1, keepdims=True) for s in sc_m]
    e_sel = [jnp.zeros((tm, 1), F32)] * TOP_K
    w_sel = [jnp.zeros((tm, 1), F32)] * TOP_K
    for m in range(GROUP_SIZE):
        rank = jnp.zeros((tm, 1), jnp.int32)
        for j in range(GROUP_SIZE):
            if j != m:
                ahead = (val[j] > val[m]) | ((val[j] == val[m]) & (j < m))
                rank = rank + ahead.astype(jnp.int32)
        for kk in range(TOP_K):
            e_sel[kk] = jnp.where(rank == kk, (best * GROUP_SIZE + m).astype(F32), e_sel[kk])
            w_sel[kk] = jnp.where(rank == kk, aff[m], w_sel[kk])
    tot = w_sel[0] + w_sel[1]

    lane_f = lane.astype(F32)
    chosen = [lane_f == e_sel[kk] for kk in range(TOP_K)]
    onehot = jnp.where(chosen[0] | chosen[1], 1.0, 0.0).astype(BF16)
    running = _dot(tril_ref[...], onehot)
    rank = [jnp.sum(jnp.where(chosen[kk], running, 0.0), axis=-1, keepdims=True) - 1.0 for kk in range(TOP_K)]
    count_ref[0] = running[tm - 1:tm, :]
    route_ref[...] = jnp.where(lane == 0, e_sel[0], jnp.where(lane == 1, e_sel[1], jnp.where(
        lane == 2, w_sel[0] / tot, jnp.where(lane == 3, w_sel[1] / tot, jnp.where(
            lane == 4, rank[0], jnp.where(lane == 5, rank[1], 0.0))))))


def _merge(h, ya, yb, o_f, o_b, z, layer_weights, layer, gn, seg, lg, lb, wr_hi, wr_lo, rb):
    t, d = h.shape
    tm = TOKEN_TILE

    def rows(w):
        return pl.BlockSpec((tm, w), lambda i: (i, 0))

    def const(a):
        return pl.BlockSpec(a.shape, lambda i: (0, 0))

    i = np.arange(tm)
    tril = jnp.asarray((i[None, :] <= i[:, None]).astype(np.float32), BF16)
    consts = (gn, seg, lg, lb, wr_hi, wr_lo, rb, tril)
    return pl.pallas_call(
        _merge_kernel, grid=(t // tm,),
        in_specs=([rows(d), rows(A_Q_W), rows(B_W), rows(C_W), rows(C_W), rows(C_W)]
                  + [_layer_block(w, layer) for w in layer_weights] + [const(a) for a in consts]),
        out_specs=[rows(d), rows(d), rows(LANES), pl.BlockSpec((1, 1, LANES), lambda i: (i, 0, 0))],
        out_shape=[jax.ShapeDtypeStruct((t, d), F32), jax.ShapeDtypeStruct((t, d), BF16),
                   jax.ShapeDtypeStruct((t, LANES), F32), jax.ShapeDtypeStruct((t // tm, 1, LANES), F32)],
        compiler_params=_cparams("parallel"), name="merge_outproj_ln_router",
    )(h, ya, yb, o_f, o_b, z, *layer_weights, *consts)


def _expert_kernel(te_ref, nv_ref, x_ref, w1_ref, w3_ref, w2_ref, y_ref, w1b_ref, w3b_ref, w2b_ref):
    i = pl.program_id(0)

    @pl.when((i == 0) | (te_ref[i] != te_ref[jnp.maximum(i - 1, 0)]))
    def _():
        w1b_ref[...] = w1_ref[0, 0].astype(BF16)
        w3b_ref[...] = w3_ref[0, 0].astype(BF16)
        w2b_ref[...] = w2_ref[0, 0].astype(BF16)

    @pl.when(i < nv_ref[0])
    def _():
        x = x_ref[...]
        hid = _silu(_dot(x, w1b_ref[...])) * _dot(x, w3b_ref[...])
        y_ref[...] = _dot(hid.astype(BF16), w2b_ref[...]).astype(BF16)

    @pl.when(i >= nv_ref[0])
    def _():
        y_ref[...] = jnp.zeros_like(y_ref)


def _expert_mlp(xs, tile_expert, n_valid, w1, w3, w2, layer):
    mp, d = xs.shape
    te = EXPERT_TILE
    grid_spec = pltpu.PrefetchScalarGridSpec(
        num_scalar_prefetch=2, grid=(mp // te,),
        in_specs=[pl.BlockSpec((te, d), lambda i, e, n: (i, 0)),
                  pl.BlockSpec((1, 1, d, D_EXPERT), lambda i, e, n: (layer, e[i], 0, 0)),
                  pl.BlockSpec((1, 1, d, D_EXPERT), lambda i, e, n: (layer, e[i], 0, 0)),
                  pl.BlockSpec((1, 1, D_EXPERT, d), lambda i, e, n: (layer, e[i], 0, 0))],
        out_specs=pl.BlockSpec((te, d), lambda i, e, n: (i, 0)),
        scratch_shapes=[pltpu.VMEM((d, D_EXPERT), BF16), pltpu.VMEM((d, D_EXPERT), BF16),
                        pltpu.VMEM((D_EXPERT, d), BF16)])
    return pl.pallas_call(
        _expert_kernel, grid_spec=grid_spec, out_shape=jax.ShapeDtypeStruct((mp, d), BF16),
        compiler_params=_cparams("arbitrary"), name="expert_mlp",
    )(tile_expert, n_valid, xs, w1, w3, w2)


def _dispatch_plan(route, tile_counts, t):
    te = EXPERT_TILE
    e = route[:, 0:TOP_K].astype(jnp.int32)
    rank = route[:, 2 * TOP_K:3 * TOP_K].astype(jnp.int32)
    tile_counts = tile_counts[:, 0, :N_EXPERTS].astype(jnp.int32)
    before = jnp.cumsum(tile_counts, axis=0) - tile_counts
    counts = tile_counts.sum(axis=0)
    padded = ((counts + te - 1) // te) * te
    ends = jnp.cumsum(padded)
    base = (ends - padded)[None, :] + before
    base_tok = jnp.repeat(base, t // tile_counts.shape[0], axis=0)
    chosen = e[:, :, None] == jnp.arange(N_EXPERTS, dtype=jnp.int32)[None, None, :]
    pos = jnp.sum(jnp.where(chosen, base_tok[:, None, :], 0), axis=-1) + rank
    n_tiles = (t * TOP_K) // te + N_EXPERTS
    tile_start = jnp.arange(n_tiles, dtype=jnp.int32) * te
    tile_expert = jnp.minimum((tile_start[:, None] >= ends[None, :]).astype(jnp.int32).sum(axis=1), N_EXPERTS - 1)
    n_valid = (ends[-1] // te).astype(jnp.int32).reshape(1)
    src = jnp.zeros((n_tiles * te,), jnp.int32).at[pos.reshape(-1)].set(
        jnp.repeat(jnp.arange(t, dtype=jnp.int32), TOP_K))
    return pos, src, tile_expert, n_valid


def _combine_kernel(h_ref, y0_ref, y1_ref, route_ref, g_ref, b_ref, o_ref):
    lane = lax.broadcasted_iota(jnp.int32, route_ref.shape, 1)
    r = route_ref[...]
    w0 = jnp.sum(jnp.where(lane == 2, r, 0.0), axis=-1, keepdims=True)
    w1 = jnp.sum(jnp.where(lane == 3, r, 0.0), axis=-1, keepdims=True)
    y = w0 * y0_ref[...].astype(F32) + w1 * y1_ref[...].astype(F32)
    o_ref[...] = _layernorm_rows(DN_ALPHA * h_ref[...] + y, g_ref[...], b_ref[...])


def _combine(h1, y0, y1, route, g, b):
    t, d = h1.shape
    row = pl.BlockSpec((TOKEN_TILE, d), lambda i: (i, 0))
    vec = pl.BlockSpec((1, d), lambda i: (0, 0))
    return pl.pallas_call(
        _combine_kernel, grid=(t // TOKEN_TILE,),
        in_specs=[row, row, row, pl.BlockSpec((TOKEN_TILE, LANES), lambda i: (i, 0)), vec, vec],
        out_specs=row, out_shape=jax.ShapeDtypeStruct((t, d), F32),
        compiler_params=_cparams("parallel"), name="combine_ln",
    )(h1, y0, y1, route, g.reshape(1, d), b.reshape(1, d))


def _rope_tables(seq):
    tkn = np.arange(seq)
    pos = np.stack([tkn // GRID_W, tkn % GRID_W], axis=1).astype(np.float32)
    half = HEAD_DIM // 2
    inv = ROPE_THETA ** (-np.arange(0, half, 2, dtype=np.float32) / half)
    d = np.arange(LANES) % HEAD_DIM
    axis = d // half
    r = d % half
    ang = pos[:, axis] * inv[r % (half // 2)][None, :]
    sign = np.where(r < half // 2, -1.0, 1.0)[None, :]
    return jnp.asarray(np.cos(ang), F32), jnp.asarray(np.sin(ang) * sign, F32)


def _segment_matrix(width):
    i = np.arange(width) // HEAD_DIM
    return jnp.asarray((i[:, None] == i[None, :]).astype(np.float32), BF16)


def _router_layout(w_router, router_bias):
    perm = np.array([(l % N_GROUPS) * GROUP_SIZE + l // N_GROUPS for l in range(N_EXPERTS)])
    w = jnp.pad(w_router[:, perm], ((0, 0), (0, LANES - N_EXPERTS)))
    hi = w.astype(BF16)
    lo = (w - hi.astype(F32)).astype(BF16)
    rb = jnp.pad(router_bias[perm], (0, LANES - N_EXPERTS)).reshape(1, LANES)
    return hi, lo, rb


def kernel(x, ln0_g, ln0_b, w_in, q_norm_g, k_norm_g, na_rpb, conv_w, A_log, dt_bias, gdn_norm_g,
           w_branch_a, w_branch_b, w_branch_c, w_out, ln1_g, ln1_b, w_router, router_bias, w1, w3, w2,
           ln2_g, ln2_b):
    batch, seq, d = x.shape
    t = batch * seq
    depth = w_in.shape[0]
    rows = seq // GRID_W
    cos, sin = _rope_tables(seq)
    seg128 = _segment_matrix(LANES)
    seg256 = _segment_matrix(C_W)
    wr_hi, wr_lo, rb = _router_layout(w_router, router_bias)

    w_main = w_in[:, :, :MAIN_W].astype(BF16)
    w_small = jnp.pad(w_in[:, :, MAIN_W:MAIN_W + SMALL_W], ((0, 0), (0, 0), (0, LANES - SMALL_W))).astype(BF16)
    merge_weights = (w_in[:, :, MAIN_W + SMALL_W:].astype(BF16), w_branch_a.astype(BF16),
                     w_branch_b.astype(BF16), w_branch_c.astype(BF16), w_out.astype(BF16))

    def layer(h, l, tiles):
        tg = h.shape[0]
        bg = tg // seq
        gp = jnp.zeros((8, LANES), F32)
        gp = gp.at[0, 2 * C_HEADS:4 * C_HEADS].set(A_log[l].reshape(-1))
        gp = gp.at[1, 2 * C_HEADS:4 * C_HEADS].set(dt_bias[l].reshape(-1))
        qg = jnp.tile(q_norm_g[l], LANES // HEAD_DIM).reshape(1, LANES)
        kg = jnp.tile(k_norm_g[l], LANES // HEAD_DIM).reshape(1, LANES)
        qpad, ak, av, bq, bk, bv, cqkv, cz, sm = _projection(h, w_main, w_small, l, cos, sin, qg, kg, seg128, gp, seq)

        ya = _gqa(qpad, ak, av, bg, seq)
        yb = _neighbourhood(bq, bk, bv, *tiles, bg, seq)
        conv_pad = jnp.pad(conv_w[l], ((0, 8 - CONV_K), (0, 0)))
        cq, ck, kb_f, kb_b, vb_f, vb_b, gc_f, gc_b, gcs = _short_conv_norm(cqkv, sm, conv_pad, seg256, seq)
        o_f, o_b = _gated_delta(cq, ck, kb_f, kb_b, vb_f, vb_b, gc_f, gc_b, gcs, bg, seq)

        gn = jnp.tile(gdn_norm_g[l], C_HEADS).reshape(1, C_W)
        h1, h1_bf16, route, tile_counts = _merge(h, ya, yb, o_f, o_b, cz, merge_weights, l, gn, seg256,
                                                 ln1_g[l].reshape(1, d), ln1_b[l].reshape(1, d), wr_hi, wr_lo, rb)

        pos, src, tile_expert, n_valid = _dispatch_plan(route, tile_counts, tg)
        xs = jnp.take(h1_bf16, src, axis=0, mode="clip")
        ys = _expert_mlp(xs, tile_expert, n_valid, w1, w3, w2, l)
        y0 = jnp.take(ys, pos[:, 0], axis=0, mode="clip")
        y1 = jnp.take(ys, pos[:, 1], axis=0, mode="clip")
        return _combine(h1, y0, y1, route, ln2_g[l], ln2_b[l])

    n_groups = SEQUENCE_GROUPS if batch % SEQUENCE_GROUPS == 0 else 1
    h = _input_layernorm(x.reshape(t, d), ln0_g, ln0_b)
    groups = [h[g * (t // n_groups):(g + 1) * (t // n_groups)] for g in range(n_groups)]
    for l in range(depth):
        tiles = _na_bias_tiles(na_rpb[l], rows)
        groups = [layer(hg, l, tiles) for hg in groups]
    return jnp.concatenate(groups, axis=0).reshape(batch, seq, d)
```

```python
import functools
import math

import numpy as np
import jax
import jax.numpy as jnp
from jax import lax
from jax.experimental import pallas as pl
from jax.experimental.pallas import tpu as pltpu

F32 = jnp.float32
BF16 = jnp.bfloat16

D_MODEL = 1024
GRID_W = 64
HEAD_DIM = 64
A_Q_HEADS = 8
A_KV_HEADS = 2
ROPE_THETA = 10000.0
B_HEADS = 4
WIN_R = 8
WIN_C = 16
C_HEADS = 4
CONV_K = 5
N_EXPERTS = 32
N_GROUPS = 8
GROUP_SIZE = N_EXPERTS // N_GROUPS
TOP_K = 2
D_EXPERT = 512
DEPTH = 4
DN_ALPHA = (2 * DEPTH) ** 0.25

A_Q_W = A_Q_HEADS * HEAD_DIM
A_KV_W = A_KV_HEADS * HEAD_DIM
B_W = B_HEADS * HEAD_DIM
C_W = C_HEADS * HEAD_DIM
MAIN_W = A_Q_W + 2 * A_KV_W + 3 * B_W + 4 * C_W
SMALL_W = 4 * C_HEADS

LANES = 128
TOKEN_TILE = 512
ATTN_Q_TILE = 512
NA_ROWS_PER_BLOCK = 8
GDN_CHUNK = 64
GDN_BLOCK = 512
EXPERT_TILE = 256
SEQUENCE_GROUPS = 2
VMEM_LIMIT = 48 * 1024 * 1024
NEG_BIG = -1e30
LOG2_E = math.log2(math.e)
HIGHEST = lax.Precision.HIGHEST


def _cparams(*sem):
    return pltpu.CompilerParams(dimension_semantics=sem, vmem_limit_bytes=VMEM_LIMIT)


def _dot(a, b):
    return jnp.dot(a, b, preferred_element_type=F32)


def _dot_nt(a, b):
    return lax.dot_general(a, b, (((1,), (1,)), ((), ())), preferred_element_type=F32)


def _dot_tn(a, b):
    return lax.dot_general(a, b, (((0,), (0,)), ((), ())), preferred_element_type=F32)


def _dot_exact(a, b):
    return jnp.dot(a, b, preferred_element_type=F32, precision=HIGHEST)


def _seg_sum(y, seg):
    hi = y.astype(BF16)
    lo = (y - hi.astype(F32)).astype(BF16)
    return _dot(hi, seg) + _dot(lo, seg)


def _sigmoid(x):
    return 1.0 / (1.0 + jnp.exp(-x))


def _silu(x):
    return x * _sigmoid(x)


def _layernorm_rows(y, g, b):
    mu = jnp.mean(y, axis=-1, keepdims=True)
    yc = y - mu
    var = jnp.mean(yc * yc, axis=-1, keepdims=True)
    return yc * lax.rsqrt(var + 1e-5) * g + b


def _ln_kernel(x_ref, g_ref, b_ref, o_ref):
    o_ref[...] = _layernorm_rows(x_ref[...], g_ref[...], b_ref[...])


def _input_layernorm(x, g, b):
    t, d = x.shape
    row = pl.BlockSpec((TOKEN_TILE, d), lambda i: (i, 0))
    vec = pl.BlockSpec((1, d), lambda i: (0, 0))
    return pl.pallas_call(
        _ln_kernel, grid=(t // TOKEN_TILE,), in_specs=[row, vec, vec], out_specs=row,
        out_shape=jax.ShapeDtypeStruct((t, d), F32), compiler_params=_cparams("parallel"),
        name="input_layernorm")(x, g.reshape(1, d), b.reshape(1, d))


def _proj_kernel(h_ref, w_ref, ws_ref, cos_ref, sin_ref, qg_ref, kg_ref, seg_ref, gp_ref,
                 q_ref, k_ref, v_ref, bq_ref, bk_ref, bv_ref, c_ref, z_ref, sm_ref):
    x = h_ref[...].astype(BF16)
    tm = x.shape[0]
    lane = lax.broadcasted_iota(jnp.int32, (tm, LANES), 1)
    first_half = (lane % (HEAD_DIM // 2)) < (HEAD_DIM // 4)
    low_head = lane < HEAD_DIM
    cos = cos_ref[...]
    sin = sin_ref[...]
    seg = seg_ref[...]

    def mm(a, b):
        return _dot(x, w_ref[0, :, a:b])

    def norm_rope(y, g):
        ms = _seg_sum(y * y, seg) * (1.0 / HEAD_DIM)
        yn = y * lax.rsqrt(ms + 1e-6) * g
        partner = jnp.where(first_half, pltpu.roll(yn, LANES - HEAD_DIM // 4, 1),
                            pltpu.roll(yn, HEAD_DIM // 4, 1))
        return yn * cos + partner * sin

    for j in range(A_Q_W // LANES):
        y = norm_rope(mm(j * LANES, (j + 1) * LANES), qg_ref[...]) * (LOG2_E * HEAD_DIM ** -0.5)
        yr = pltpu.roll(y, HEAD_DIM, 1)
        kv_head = (2 * j) // (A_Q_HEADS // A_KV_HEADS)
        if kv_head == 0:
            even, odd = jnp.where(low_head, y, 0.0), jnp.where(low_head, yr, 0.0)
        else:
            even, odd = jnp.where(low_head, 0.0, yr), jnp.where(low_head, 0.0, y)
        q_ref[:, (2 * j) * LANES:(2 * j + 1) * LANES] = even.astype(BF16)
        q_ref[:, (2 * j + 1) * LANES:(2 * j + 2) * LANES] = odd.astype(BF16)
    o = A_Q_W
    k_ref[...] = norm_rope(mm(o, o + A_KV_W), kg_ref[...]).astype(BF16)
    o += A_KV_W
    v_ref[:, 0:A_KV_W] = mm(o, o + A_KV_W).astype(BF16)
    v_ref[:, A_KV_W:2 * A_KV_W] = jnp.ones((tm, A_KV_W), BF16)
    o += A_KV_W
    bq_ref[...] = (mm(o, o + B_W) * (HEAD_DIM ** -0.5)).astype(BF16)
    o += B_W
    bk_ref[...] = mm(o, o + B_W).astype(BF16)
    o += B_W
    bv_ref[...] = mm(o, o + B_W).astype(BF16)
    o += B_W
    c_ref[...] = mm(o, o + 3 * C_W)
    o += 3 * C_W
    z_ref[...] = mm(o, o + C_W).astype(BF16)

    s = _dot(x, ws_ref[0])
    a = jnp.exp(gp_ref[0:1, :])
    u = s + gp_ref[1:2, :]
    softplus = jnp.maximum(u, 0.0) + jnp.log(1.0 + jnp.exp(-jnp.abs(u)))
    sm_ref[...] = jnp.where(lane < 2 * C_HEADS, _sigmoid(s), jnp.where(lane < SMALL_W, -a * softplus, 0.0))


def _layer_block(a, layer):
    return pl.BlockSpec((1,) + a.shape[1:], lambda i: (layer,) + (0,) * (a.ndim - 1))


def _projection(h, w_main, w_small, layer, cos, sin, qg, kg, seg, gp, seq):
    t, d = h.shape
    tm = TOKEN_TILE
    nseq = seq // tm

    def rows(w):
        return pl.BlockSpec((tm, w), lambda i: (i, 0))

    def const(shape):
        return pl.BlockSpec(shape, lambda i: (0, 0))

    pos = pl.BlockSpec((tm, LANES), lambda i: (i % nseq, 0))
    out_w = [(2 * A_Q_W, BF16), (A_KV_W, BF16), (2 * A_KV_W, BF16), (B_W, BF16), (B_W, BF16), (B_W, BF16),
             (3 * C_W, F32), (C_W, BF16), (LANES, F32)]
    return pl.pallas_call(
        _proj_kernel, grid=(t // tm,),
        in_specs=[rows(d), _layer_block(w_main, layer), _layer_block(w_small, layer), pos, pos, const((1, LANES)),
                  const((1, LANES)), const((LANES, LANES)), const((8, LANES))],
        out_specs=[rows(w) for w, _ in out_w],
        out_shape=[jax.ShapeDtypeStruct((t, w), dt) for w, dt in out_w],
        compiler_params=_cparams("parallel"), name="projection",
    )(h, w_main, w_small, cos, sin, qg, kg, seg, gp)


def _gqa_kernel(q_ref, k_ref, v1_ref, o_ref):
    tq = q_ref.shape[0]
    lane = lax.broadcasted_iota(jnp.int32, (tq, LANES), 1)
    low_head = lane < HEAD_DIM
    k = k_ref[...]
    v1 = v1_ref[...]
    group = A_Q_HEADS // A_KV_HEADS
    outs = []
    for h in range(A_Q_HEADS):
        s = _dot_nt(q_ref[:, h * LANES:(h + 1) * LANES], k)
        p = jnp.exp2(s - jnp.max(s, axis=-1, keepdims=True)).astype(BF16)
        acc = _dot(p, v1)
        outs.append(acc[:, 0:A_KV_W] / acc[:, A_KV_W:A_KV_W + 1])
    for j in range(A_Q_HEADS // 2):
        even, odd = outs[2 * j], outs[2 * j + 1]
        if (2 * j) // group == 0:
            blk = jnp.where(low_head, even, pltpu.roll(odd, HEAD_DIM, 1))
        else:
            blk = jnp.where(low_head, pltpu.roll(even, HEAD_DIM, 1), odd)
        o_ref[:, j * LANES:(j + 1) * LANES] = blk.astype(BF16)


def _gqa(qpad, k, v1, batch, seq):
    t = qpad.shape[0]
    tq = ATTN_Q_TILE
    nq = seq // tq
    return pl.pallas_call(
        _gqa_kernel, grid=(batch, nq),
        in_specs=[pl.BlockSpec((tq, 2 * A_Q_W), lambda b, i: (b * nq + i, 0)),
                  pl.BlockSpec((seq, A_KV_W), lambda b, i: (b, 0)),
                  pl.BlockSpec((seq, 2 * A_KV_W), lambda b, i: (b, 0))],
        out_specs=pl.BlockSpec((tq, A_Q_W), lambda b, i: (b * nq + i, 0)),
        out_shape=jax.ShapeDtypeStruct((t, A_Q_W), BF16),
        compiler_params=_cparams("parallel", "parallel"), name="gqa_attention",
    )(qpad, k, v1)


def _na_key_row0(qb, rows):
    return min(max(NA_ROWS_PER_BLOCK * qb - WIN_R // 2, 0), rows - 2 * NA_ROWS_PER_BLOCK)


def _na_kernel(which_ref, q_ref, k_ref, v_ref, pair_ref, o_ref, bias_ref, *, rows):
    qb = pl.program_id(0)
    nq = q_ref.shape[0]
    nk = 2 * NA_ROWS_PER_BLOCK * GRID_W

    @pl.when(pl.program_id(1) == 0)
    def _():
        n = NA_ROWS_PER_BLOCK
        for rq in range(n):
            for rp in range(n):
                tile = which_ref[(qb * n + rq) * n + rp]
                for h in range(B_HEADS):
                    bias_ref[h, rq * GRID_W:(rq + 1) * GRID_W, rp * LANES:(rp + 1) * LANES] = pair_ref[h, tile]

    row0 = jnp.clip(NA_ROWS_PER_BLOCK * qb - WIN_R // 2, 0, rows - 2 * NA_ROWS_PER_BLOCK)
    start = pl.multiple_of(row0 * GRID_W, 4 * GRID_W)
    lane = lax.broadcasted_iota(jnp.int32, (nq, LANES), 1)
    low_head = lane < HEAD_DIM
    zero = jnp.zeros((), BF16)
    for j in range(B_W // LANES):
        kblk = k_ref[pl.ds(start, nk), j * LANES:(j + 1) * LANES]
        vblk = v_ref[pl.ds(start, nk), j * LANES:(j + 1) * LANES]
        qblk = q_ref[:, j * LANES:(j + 1) * LANES]
        outs = []
        for par in range(2):
            qh = jnp.where(low_head if par == 0 else jnp.logical_not(low_head), qblk, zero)
            s = _dot_nt(qh, kblk) + bias_ref[2 * j + par]
            m = jnp.max(s, axis=-1, keepdims=True)
            p = jnp.exp(s - m)
            l = jnp.sum(p, axis=-1, keepdims=True)
            outs.append(_dot(p.astype(BF16), vblk) / l)
        o_ref[:, j * LANES:(j + 1) * LANES] = jnp.where(low_head, outs[0], outs[1]).astype(BF16)


def _na_bias_tiles(rpb, rows):
    nqb = rows // NA_ROWS_PER_BLOCK
    nkr = 2 * NA_ROWS_PER_BLOCK
    wr = min(WIN_R, rows)
    n_dr = 2 * WIN_R - 1
    c = np.arange(GRID_W)
    c0 = np.clip(c - WIN_C // 2, 0, GRID_W - WIN_C)
    in_c = (c[None, :] >= c0[:, None]) & (c[None, :] < c0[:, None] + WIN_C)
    dc = np.clip(c[None, :] - c[:, None] + (WIN_C - 1), 0, 2 * WIN_C - 2)
    col_tiles = jnp.where(jnp.asarray(in_c)[None, None], rpb[:, :, jnp.asarray(dc)], NEG_BIG)
    col_tiles = jnp.concatenate([col_tiles, jnp.full((B_HEADS, 1, GRID_W, GRID_W), NEG_BIG, F32)], axis=1)
    tile_idx = np.full((nqb, NA_ROWS_PER_BLOCK, nkr), n_dr, np.int32)
    for qb in range(nqb):
        for rq in range(NA_ROWS_PER_BLOCK):
            r = NA_ROWS_PER_BLOCK * qb + rq
            r0 = min(max(r - wr // 2, 0), rows - wr)
            for rk in range(nkr):
                kr = _na_key_row0(qb, rows) + rk
                if r0 <= kr < r0 + wr:
                    tile_idx[qb, rq, rk] = kr - r + (WIN_R - 1)
    pairs, which = np.unique(tile_idx.reshape(-1, 2), axis=0, return_inverse=True)
    pair_tiles = jnp.concatenate([jnp.take(col_tiles, jnp.asarray(pairs[:, 0]), axis=1),
                                  jnp.take(col_tiles, jnp.asarray(pairs[:, 1]), axis=1)], axis=-1)
    return pair_tiles, jnp.asarray(which.reshape(-1).astype(np.int32))


def _neighbourhood(q, k, v, pair_tiles, which, batch, seq):
    t = q.shape[0]
    rows = seq // GRID_W
    nqb = rows // NA_ROWS_PER_BLOCK
    nq = NA_ROWS_PER_BLOCK * GRID_W
    kv = pl.BlockSpec((seq, B_W), lambda qb, b, w: (b, 0))
    qo = pl.BlockSpec((nq, B_W), lambda qb, b, w: (b * nqb + qb, 0))
    grid_spec = pltpu.PrefetchScalarGridSpec(
        num_scalar_prefetch=1, grid=(nqb, batch),
        in_specs=[qo, kv, kv, pl.BlockSpec(pair_tiles.shape, lambda qb, b, w: (0, 0, 0, 0))],
        out_specs=qo, scratch_shapes=[pltpu.VMEM((B_HEADS, nq, 2 * nq), F32)])
    return pl.pallas_call(
        functools.partial(_na_kernel, rows=rows), grid_spec=grid_spec,
        out_shape=jax.ShapeDtypeStruct((t, B_W), BF16),
        compiler_params=_cparams("arbitrary", "arbitrary"), name="neighbourhood_attention",
    )(which, q, k, v, pair_tiles)


def _dot_select(x, sel):
    hi = x.astype(BF16)
    r1 = x - hi.astype(F32)
    mid = r1.astype(BF16)
    lo = (r1 - mid.astype(F32)).astype(BF16)
    return _dot(hi, sel) + _dot(mid, sel) + _dot(lo, sel)


def _conv_kernel(c_ref, prev_ref, next_ref, w_ref, seg_ref, sm_ref, e_ref, cum_f_ref, cum_b_ref,
                 q_ref, k_ref, kbf_ref, kbb_ref, vbf_ref, vbb_ref, gcf_ref, gcb_ref, gcs_ref, pad_ref, *,
                 tiles_per_seq):
    i = pl.program_id(0)
    tm = c_ref.shape[0]
    halo = prev_ref.shape[0]
    first = (i % tiles_per_seq) == 0
    last = (i % tiles_per_seq) == tiles_per_seq - 1
    pad_ref[0:halo, :] = jnp.where(first, 0.0, prev_ref[...])
    pad_ref[halo:halo + tm, :] = c_ref[...]
    pad_ref[halo + tm:2 * halo + tm, :] = jnp.where(last, 0.0, next_ref[...])
    acc = None
    for j in range(CONV_K):
        o = halo - CONV_K // 2 + j
        term = w_ref[j:j + 1, :] * pad_ref[o:o + tm, :]
        acc = term if acc is None else acc + term
    y = _silu(acc)
    seg = seg_ref[...]

    def l2n(u):
        return u * lax.rsqrt(_seg_sum(u * u, seg) + 1e-6)

    q_ref[...] = (l2n(y[:, 0:C_W]) * (HEAD_DIM ** -0.5)).astype(BF16)
    k = l2n(y[:, C_W:2 * C_W])
    v = y[:, 2 * C_W:3 * C_W]
    k_ref[...] = k.astype(BF16)

    sm = sm_ref[...]
    beta_f = _dot_select(sm, e_ref[:, 0 * C_W:1 * C_W])
    beta_b = _dot_select(sm, e_ref[:, 1 * C_W:2 * C_W])
    kbf_ref[...] = (k * beta_f).astype(BF16)
    kbb_ref[...] = (k * beta_b).astype(BF16)
    vbf_ref[...] = (v * beta_f).astype(BF16)
    vbb_ref[...] = (v * beta_b).astype(BF16)
    cum_f = _dot_select_lhs(cum_f_ref[...], sm)
    cum_b = _dot_select_lhs(cum_b_ref[...], sm)
    gcf_ref[...] = _dot_select(cum_f, e_ref[:, 2 * C_W:3 * C_W])
    gcb_ref[...] = _dot_select(cum_b, e_ref[:, 3 * C_W:4 * C_W])
    lane = lax.broadcasted_iota(jnp.int32, sm.shape, 1)
    gcs_ref[...] = jnp.where(lane < 3 * C_HEADS, cum_f, cum_b)


def _dot_select_lhs(sel, x):
    hi = x.astype(BF16)
    r1 = x - hi.astype(F32)
    mid = r1.astype(BF16)
    lo = (r1 - mid.astype(F32)).astype(BF16)
    return _dot(sel, hi) + _dot(sel, mid) + _dot(sel, lo)


def _short_conv_norm(c, sm, conv_w, seg, seq):
    t, w = c.shape
    tm = TOKEN_TILE
    halo = 8
    nb = tm // halo
    last_blk = t // halo - 1
    e, cum_f, cum_b = _gdn_constants(tm)
    out = pl.BlockSpec((tm, C_W), lambda i: (i, 0))

    def const(a):
        return pl.BlockSpec(a.shape, lambda i: (0, 0))

    return pl.pallas_call(
        functools.partial(_conv_kernel, tiles_per_seq=seq // tm), grid=(t // tm,),
        in_specs=[pl.BlockSpec((tm, w), lambda i: (i, 0)),
                  pl.BlockSpec((halo, w), lambda i: (jnp.maximum(i * nb - 1, 0), 0)),
                  pl.BlockSpec((halo, w), lambda i: (jnp.minimum((i + 1) * nb, last_blk), 0)),
                  pl.BlockSpec((8, w), lambda i: (0, 0)),
                  pl.BlockSpec((C_W, C_W), lambda i: (0, 0)),
                  pl.BlockSpec((tm, LANES), lambda i: (i, 0)), const(e), const(cum_f), const(cum_b)],
        out_specs=[out] * 8 + [pl.BlockSpec((tm, LANES), lambda i: (i, 0))],
        out_shape=([jax.ShapeDtypeStruct((t, C_W), BF16)] * 6 + [jax.ShapeDtypeStruct((t, C_W), F32)] * 2
                   + [jax.ShapeDtypeStruct((t, LANES), F32)]),
        scratch_shapes=[pltpu.VMEM((tm + 2 * halo, w), F32)],
        compiler_params=_cparams("parallel"), name="gdn_conv_norm",
    )(c, c, c, conv_w, seg, sm, e, cum_f, cum_b)


def _block_diag(x, same_head):
    return jnp.where(same_head, jnp.concatenate([x] * C_HEADS, axis=0), 0.0)


def _gdn_prepare(chains):
    C = GDN_CHUNK
    ri = lax.broadcasted_iota(jnp.int32, (C, C_W), 0)
    lj = lax.broadcasted_iota(jnp.int32, (C, C_W), 1) % C
    rr = lax.broadcasted_iota(jnp.int32, (C_HEADS * C, C_W), 0) // C
    ll = lax.broadcasted_iota(jnp.int32, (C_HEADS * C, C_W), 1) // HEAD_DIM
    same_head = rr == ll
    eye = jnp.where(ri == lj, 1.0, 0.0)

    def bd(x):
        return _block_diag(x.astype(BF16), same_head)

    n = range(len(chains))
    decay = [jnp.exp(jnp.where((ri <= lj) if ch["reverse"] else (ri >= lj), ch["gc"] - ch["grow"], -jnp.inf))
             for ch in chains]
    r = [_dot_nt(jnp.concatenate([ch["kb"], ch["q"]], axis=0), bd(ch["k"])) for ch in chains]
    m = [jnp.where((ri < lj) if chains[i]["reverse"] else (ri > lj), r[i][:C] * decay[i], 0.0) for i in n]
    qk = [(r[i][C:] * decay[i]).astype(BF16) for i in n]

    p = [eye - m[i] for i in n]
    a = [_dot(m[i].astype(BF16), bd(m[i])) for i in n]
    for _ in range(int(math.log2(C)) - 2):
        r2 = [_dot(jnp.concatenate([a[i], p[i]], axis=0).astype(BF16), bd(a[i])) for i in n]
        a = [r2[i][:C] for i in n]
        p = [p[i] + r2[i][C:] for i in n]
    t_inv = [(p[i] + _dot(p[i].astype(BF16), bd(a[i]))).astype(BF16) for i in n]

    egc = [jnp.exp(ch["gc"]) for ch in chains]
    uw = [_dot(t_inv[i], jnp.concatenate([bd(chains[i]["vb"]), bd(chains[i]["kb"].astype(F32) * egc[i])], axis=1))
          for i in n]
    out = []
    for i in n:
        ch = chains[i]
        gc = ch["gc"]
        g_last = gc[0:1, :] if ch["reverse"] else gc[C - 1:C, :]
        ke = (ch["k"].astype(F32) * jnp.exp(g_last - gc)).astype(BF16)
        ab = _dot_tn(ke, uw[i].astype(BF16))
        u, w = uw[i][:, :C_W], uw[i][:, C_W:]
        qo = _dot(qk[i], jnp.concatenate([bd(u), bd(w)], axis=1))
        out.append(dict(
            a=jnp.where(same_head, -ab[:, C_W:], 0.0).astype(BF16),
            b=jnp.where(same_head, ab[:, :C_W], 0.0),
            d=jnp.exp(g_last),
            qeff=(ch["q"].astype(F32) * egc[i] - qo[:, C_W:]).astype(BF16),
            oloc=qo[:, :C_W]))
    return out


GDN_PREP_CHUNKS = 2


def _gdn_kernel(qf, kf, kbf, vbf, gcf, gwf, qb, kb, kbb, vbb, gcb, gwb, of_ref, ob_ref,
                state_ref, a_ref, b_ref, d_ref, qeff_ref, oloc_ref):
    C = GDN_CHUNK
    n = GDN_BLOCK // C
    dirs = ((qf, kf, kbf, vbf, gcf, gwf, False), (qb, kb, kbb, vbb, gcb, gwb, True))

    @pl.when(pl.program_id(1) == 0)
    def _():
        state_ref[...] = jnp.zeros_like(state_ref)

    def prepare(it, carry):
        chains, where = [], []
        for d, (q, k, kbeta, vbeta, gc, gw, reverse) in enumerate(dirs):
            for j in range(GDN_PREP_CHUNKS):
                c = it * GDN_PREP_CHUNKS + j
                sl = pl.ds(pl.multiple_of(c * C, C), C)
                chains.append(dict(q=q[sl, :], k=k[sl, :], kb=kbeta[sl, :], vb=vbeta[sl, :], gc=gc[sl, :],
                                   grow=gw[0, pl.ds(c, 1), :], reverse=reverse))
                where.append((d, c))
        for (d, c), res in zip(where, _gdn_prepare(chains)):
            a_ref[d, c] = res["a"]
            b_ref[d, c] = res["b"]
            d_ref[d, c] = res["d"]
            qeff_ref[d, c] = res["qeff"]
            oloc_ref[d, c] = res["oloc"]
        return carry

    lax.fori_loop(0, n // GDN_PREP_CHUNKS, prepare, 0)

    def scan(step, carry):
        cs = (step, n - 1 - step)
        states = [state_ref[d] for d in range(2)]
        lhs = [jnp.concatenate([a_ref[d, cs[d]], qeff_ref[d, cs[d]]], axis=0) for d in range(2)]
        res = [_dot(lhs[d], states[d].astype(BF16)) for d in range(2)]
        for d, o_ref in enumerate((of_ref, ob_ref)):
            state_ref[d] = states[d] * d_ref[d, cs[d]] + res[d][:C_W] + b_ref[d, cs[d]]
            o_ref[pl.ds(pl.multiple_of(cs[d] * C, C), C), :] = res[d][C_W:] + oloc_ref[d, cs[d]]
        return carry

    lax.fori_loop(0, n, scan, 0)


def _gdn_constants(tile):
    C = GDN_CHUNK
    e = np.zeros((LANES, 4 * C_W), np.float32)
    for kind in range(4):
        for h in range(C_HEADS):
            e[kind * C_HEADS + h, kind * C_W + h * HEAD_DIM:kind * C_W + (h + 1) * HEAD_DIM] = 1.0
    i = np.arange(C)
    tril = (i[None, :] <= i[:, None]).astype(np.float32)
    cum_f = np.kron(np.eye(tile // C, dtype=np.float32), tril)
    cum_b = np.kron(np.eye(tile // C, dtype=np.float32), tril.T)
    return tuple(jnp.asarray(a, BF16) for a in (e, cum_f, cum_b))


def _gated_delta(q, k, kb_f, kb_b, vb_f, vb_b, gc_f, gc_b, gcs, batch, seq):
    t = q.shape[0]
    nb = seq // GDN_BLOCK
    nc = GDN_BLOCK // GDN_CHUNK

    def rows(col0):
        g = gcs[:, col0:col0 + C_HEADS].reshape(t // GDN_CHUNK, GDN_CHUNK, C_HEADS)
        return jnp.transpose(g, (0, 2, 1)).reshape(t // GDN_BLOCK, nc, C_W)

    def fwd(w):
        return pl.BlockSpec((GDN_BLOCK, w), lambda b, s: (b * nb + s, 0))

    def bwd(w):
        return pl.BlockSpec((GDN_BLOCK, w), lambda b, s: (b * nb + nb - 1 - s, 0))

    gf_spec = pl.BlockSpec((1, nc, C_W), lambda b, s: (b * nb + s, 0, 0))
    gb_spec = pl.BlockSpec((1, nc, C_W), lambda b, s: (b * nb + nb - 1 - s, 0, 0))
    return pl.pallas_call(
        _gdn_kernel, grid=(batch, nb),
        in_specs=[fwd(C_W)] * 5 + [gf_spec] + [bwd(C_W)] * 5 + [gb_spec],
        out_specs=[fwd(C_W), bwd(C_W)],
        out_shape=[jax.ShapeDtypeStruct((t, C_W), F32)] * 2,
        scratch_shapes=[pltpu.VMEM((2, C_W, C_W), F32),
                        pltpu.VMEM((2, nc, C_W, C_W), BF16), pltpu.VMEM((2, nc, C_W, C_W), F32),
                        pltpu.VMEM((2, nc, 1, C_W), F32),
                        pltpu.VMEM((2, nc, GDN_CHUNK, C_W), BF16), pltpu.VMEM((2, nc, GDN_CHUNK, C_W), F32)],
        compiler_params=_cparams("parallel", "arbitrary"), name="gated_delta_rule",
    )(q, k, kb_f, vb_f, gc_f, rows(2 * C_HEADS), q, k, kb_b, vb_b, gc_b, rows(3 * C_HEADS))


def _merge_kernel(h_ref, ya_ref, yb_ref, of_ref, ob_ref, z_ref, wg_ref, wa_ref, wb_ref, wc_ref, wo_ref,
                  gn_ref, seg_ref, lg_ref, lb_ref, wrh_ref, wrl_ref, rb_ref, tril_ref,
                  h1_ref, h1b_ref, route_ref, count_ref):
    h = h_ref[...]
    x = h.astype(BF16)
    tm = h.shape[0]
    d = D_MODEL
    o = of_ref[...] + ob_ref[...]
    ms = _seg_sum(o * o, seg_ref[...]) * (1.0 / HEAD_DIM)
    yc = o * lax.rsqrt(ms + 1e-6) * gn_ref[...] * _silu(z_ref[...].astype(F32))
    mix = _sigmoid(_dot(x, wg_ref[0, :, 0:d])) * _dot(ya_ref[...], wa_ref[0])
    mix = mix + _sigmoid(_dot(x, wg_ref[0, :, d:2 * d])) * _dot(yb_ref[...], wb_ref[0])
    mix = mix + _sigmoid(_dot(x, wg_ref[0, :, 2 * d:3 * d])) * _dot(yc.astype(BF16), wc_ref[0])
    y = DN_ALPHA * h + _dot(mix.astype(BF16), wo_ref[0])
    h1 = _layernorm_rows(y, lg_ref[...], lb_ref[...])
    h1_ref[...] = h1
    h1b_ref[...] = h1.astype(BF16)

    hi = h1.astype(BF16)
    lo = (h1 - hi.astype(F32)).astype(BF16)
    logits = _dot(hi, wrh_ref[...]) + _dot(lo, wrh_ref[...]) + _dot(hi, wrl_ref[...])
    scores = _sigmoid(logits)
    sel = scores + rb_ref[...]
    lane = lax.broadcasted_iota(jnp.int32, (tm, LANES), 1)
    in_grp = lane < N_GROUPS
    sel_m = [sel if m == 0 else pltpu.roll(sel, LANES - m * N_GROUPS, 1) for m in range(GROUP_SIZE)]
    sc_m = [scores if m == 0 else pltpu.roll(scores, LANES - m * N_GROUPS, 1) for m in range(GROUP_SIZE)]
    hi01, lo01 = jnp.maximum(sel_m[0], sel_m[1]), jnp.minimum(sel_m[0], sel_m[1])
    hi23, lo23 = jnp.maximum(sel_m[2], sel_m[3]), jnp.minimum(sel_m[2], sel_m[3])
    top1 = jnp.maximum(hi01, hi23)
    top2 = jnp.maximum(jnp.minimum(hi01, hi23), jnp.maximum(lo01, lo23))
    grp_score = jnp.where(in_grp, top1 + top2, -jnp.inf)
    best_score = jnp.max(grp_score, axis=-1, keepdims=True)
    best = jnp.min(jnp.where(grp_score == best_score, lane, LANES), axis=-1, keepdims=True)
    pick = lane == best
    val = [jnp.sum(jnp.where(pick, s, 0.0), axis=-1, keepdims=True) for s in sel_m]
    aff = [jnp.sum(jnp.where(pick, s, 0.0), axis=-1, keepdims=True) for s in sc_m]
    e_sel = [jnp.zeros((tm, 1), F32)] * TOP_K
    w_sel = [jnp.zeros((tm, 1), F32)] * TOP_K
    for m in range(GROUP_SIZE):
        rank = jnp.zeros((tm, 1), jnp.int32)
        for j in range(GROUP_SIZE):
            if j != m:
                ahead = (val[j] > val[m]) | ((val[j] == val[m]) & (j < m))
                rank = rank + ahead.astype(jnp.int32)
        for kk in range(TOP_K):
            e_sel[kk] = jnp.where(rank == kk, (best * GROUP_SIZE + m).astype(F32), e_sel[kk])
            w_sel[kk] = jnp.where(rank == kk, aff[m], w_sel[kk])
    tot = w_sel[0] + w_sel[1]

    lane_f = lane.astype(F32)
    chosen = [lane_f == e_sel[kk] for kk in range(TOP_K)]
    onehot = jnp.where(chosen[0] | chosen[1], 1.0, 0.0).astype(BF16)
    running = _dot(tril_ref[...], onehot)
    rank = [jnp.sum(jnp.where(chosen[kk], running, 0.0), axis=-1, keepdims=True) - 1.0 for kk in range(TOP_K)]
    count_ref[0] = running[tm - 1:tm, :]
    route_ref[...] = jnp.where(lane == 0, e_sel[0], jnp.where(lane == 1, e_sel[1], jnp.where(
        lane == 2, w_sel[0] / tot, jnp.where(lane == 3, w_sel[1] / tot, jnp.where(
            lane == 4, rank[0], jnp.where(lane == 5, rank[1], 0.0))))))


def _merge(h, ya, yb, o_f, o_b, z, layer_weights, layer, gn, seg, lg, lb, wr_hi, wr_lo, rb):
    t, d = h.shape
    tm = TOKEN_TILE

    def rows(w):
        return pl.BlockSpec((tm, w), lambda i: (i, 0))

    def const(a):
        return pl.BlockSpec(a.shape, lambda i: (0, 0))

    i = np.arange(tm)
    tril = jnp.asarray((i[None, :] <= i[:, None]).astype(np.float32), BF16)
    consts = (gn, seg, lg, lb, wr_hi, wr_lo, rb, tril)
    return pl.pallas_call(
        _merge_kernel, grid=(t // tm,),
        in_specs=([rows(d), rows(A_Q_W), rows(B_W), rows(C_W), rows(C_W), rows(C_W)]
                  + [_layer_block(w, layer) for w in layer_weights] + [const(a) for a in consts]),
        out_specs=[rows(d), rows(d), rows(LANES), pl.BlockSpec((1, 1, LANES), lambda i: (i, 0, 0))],
        out_shape=[jax.ShapeDtypeStruct((t, d), F32), jax.ShapeDtypeStruct((t, d), BF16),
                   jax.ShapeDtypeStruct((t, LANES), F32), jax.ShapeDtypeStruct((t // tm, 1, LANES), F32)],
        compiler_params=_cparams("parallel"), name="merge_outproj_ln_router",
    )(h, ya, yb, o_f, o_b, z, *layer_weights, *consts)


def _expert_kernel(te_ref, nv_ref, x_ref, w1_ref, w3_ref, w2_ref, y_ref, w1b_ref, w3b_ref, w2b_ref):
    i = pl.program_id(0)

    @pl.when((i == 0) | (te_ref[i] != te_ref[jnp.maximum(i - 1, 0)]))
    def _():
        w1b_ref[...] = w1_ref[0, 0].astype(BF16)
        w3b_ref[...] = w3_ref[0, 0].astype(BF16)
        w2b_ref[...] = w2_ref[0, 0].astype(BF16)

    @pl.when(i < nv_ref[0])
    def _():
        x = x_ref[...]
        hid = _silu(_dot(x, w1b_ref[...])) * _dot(x, w3b_ref[...])
        y_ref[...] = _dot(hid.astype(BF16), w2b_ref[...]).astype(BF16)

    @pl.when(i >= nv_ref[0])
    def _():
        y_ref[...] = jnp.zeros_like(y_ref)


def _expert_mlp(xs, tile_expert, n_valid, w1, w3, w2, layer):
    mp, d = xs.shape
    te = EXPERT_TILE
    grid_spec = pltpu.PrefetchScalarGridSpec(
        num_scalar_prefetch=2, grid=(mp // te,),
        in_specs=[pl.BlockSpec((te, d), lambda i, e, n: (i, 0)),
                  pl.BlockSpec((1, 1, d, D_EXPERT), lambda i, e, n: (layer, e[i], 0, 0)),
                  pl.BlockSpec((1, 1, d, D_EXPERT), lambda i, e, n: (layer, e[i], 0, 0)),
                  pl.BlockSpec((1, 1, D_EXPERT, d), lambda i, e, n: (layer, e[i], 0, 0))],
        out_specs=pl.BlockSpec((te, d), lambda i, e, n: (i, 0)),
        scratch_shapes=[pltpu.VMEM((d, D_EXPERT), BF16), pltpu.VMEM((d, D_EXPERT), BF16),
                        pltpu.VMEM((D_EXPERT, d), BF16)])
    return pl.pallas_call(
        _expert_kernel, grid_spec=grid_spec, out_shape=jax.ShapeDtypeStruct((mp, d), BF16),
        compiler_params=_cparams("arbitrary"), name="expert_mlp",
    )(tile_expert, n_valid, xs, w1, w3, w2)


def _dispatch_plan(route, tile_counts, t):
    te = EXPERT_TILE
    e = route[:, 0:TOP_K].astype(jnp.int32)
    rank = route[:, 2 * TOP_K:3 * TOP_K].astype(jnp.int32)
    tile_counts = tile_counts[:, 0, :N_EXPERTS].astype(jnp.int32)
    before = jnp.cumsum(tile_counts, axis=0) - tile_counts
    counts = tile_counts.sum(axis=0)
    padded = ((counts + te - 1) // te) * te
    ends = jnp.cumsum(padded)
    base = (ends - padded)[None, :] + before
    base_tok = jnp.repeat(base, t // tile_counts.shape[0], axis=0)
    chosen = e[:, :, None] == jnp.arange(N_EXPERTS, dtype=jnp.int32)[None, None, :]
    pos = jnp.sum(jnp.where(chosen, base_tok[:, None, :], 0), axis=-1) + rank
    n_tiles = (t * TOP_K) // te + N_EXPERTS
    tile_start = jnp.arange(n_tiles, dtype=jnp.int32) * te
    tile_expert = jnp.minimum((tile_start[:, None] >= ends[None, :]).astype(jnp.int32).sum(axis=1), N_EXPERTS - 1)
    n_valid = (ends[-1] // te).astype(jnp.int32).reshape(1)
    src = jnp.zeros((n_tiles * te,), jnp.int32).at[pos.reshape(-1)].set(
        jnp.repeat(jnp.arange(t, dtype=jnp.int32), TOP_K))
    return pos, src, tile_expert, n_valid


def _combine_kernel(h_ref, y0_ref, y1_ref, route_ref, g_ref, b_ref, o_ref):
    lane = lax.broadcasted_iota(jnp.int32, route_ref.shape, 1)
    r = route_ref[...]
    w0 = jnp.sum(jnp.where(lane == 2, r, 0.0), axis=-1, keepdims=True)
    w1 = jnp.sum(jnp.where(lane == 3, r, 0.0), axis=-1, keepdims=True)
    y = w0 * y0_ref[...].astype(F32) + w1 * y1_ref[...].astype(F32)
    o_ref[...] = _layernorm_rows(DN_ALPHA * h_ref[...] + y, g_ref[...], b_ref[...])


def _combine(h1, y0, y1, route, g, b):
    t, d = h1.shape
    row = pl.BlockSpec((TOKEN_TILE, d), lambda i: (i, 0))
    vec = pl.BlockSpec((1, d), lambda i: (0, 0))
    return pl.pallas_call(
        _combine_kernel, grid=(t // TOKEN_TILE,),
        in_specs=[row, row, row, pl.BlockSpec((TOKEN_TILE, LANES), lambda i: (i, 0)), vec, vec],
        out_specs=row, out_shape=jax.ShapeDtypeStruct((t, d), F32),
        compiler_params=_cparams("parallel"), name="combine_ln",
    )(h1, y0, y1, route, g.reshape(1, d), b.reshape(1, d))


def _rope_tables(seq):
    tkn = np.arange(seq)
    pos = np.stack([tkn // GRID_W, tkn % GRID_W], axis=1).astype(np.float32)
    half = HEAD_DIM // 2
    inv = ROPE_THETA ** (-np.arange(0, half, 2, dtype=np.float32) / half)
    d = np.arange(LANES) % HEAD_DIM
    axis = d // half
    r = d % half
    ang = pos[:, axis] * inv[r % (half // 2)][None, :]
    sign = np.where(r < half // 2, -1.0, 1.0)[None, :]
    return jnp.asarray(np.cos(ang), F32), jnp.asarray(np.sin(ang) * sign, F32)


def _segment_matrix(width):
    i = np.arange(width) // HEAD_DIM
    return jnp.asarray((i[:, None] == i[None, :]).astype(np.float32), BF16)


def _router_layout(w_router, router_bias):
    perm = np.array([(l % N_GROUPS) * GROUP_SIZE + l // N_GROUPS for l in range(N_EXPERTS)])
    w = jnp.pad(w_router[:, perm], ((0, 0), (0, LANES - N_EXPERTS)))
    hi = w.astype(BF16)
    lo = (w - hi.astype(F32)).astype(BF16)
    rb = jnp.pad(router_bias[perm], (0, LANES - N_EXPERTS)).reshape(1, LANES)
    return hi, lo, rb


def kernel(x, ln0_g, ln0_b, w_in, q_norm_g, k_norm_g, na_rpb, conv_w, A_log, dt_bias, gdn_norm_g,
           w_branch_a, w_branch_b, w_branch_c, w_out, ln1_g, ln1_b, w_router, router_bias, w1, w3, w2,
           ln2_g, ln2_b):
    batch, seq, d = x.shape
    t = batch * seq
    depth = w_in.shape[0]
    rows = seq // GRID_W
    cos, sin = _rope_tables(seq)
    seg128 = _segment_matrix(LANES)
    seg256 = _segment_matrix(C_W)
    wr_hi, wr_lo, rb = _router_layout(w_router, router_bias)

    w_main = w_in[:, :, :MAIN_W].astype(BF16)
    w_small = jnp.pad(w_in[:, :, MAIN_W:MAIN_W + SMALL_W], ((0, 0), (0, 0), (0, LANES - SMALL_W))).astype(BF16)
    merge_weights = (w_in[:, :, MAIN_W + SMALL_W:].astype(BF16), w_branch_a.astype(BF16),
                     w_branch_b.astype(BF16), w_branch_c.astype(BF16), w_out.astype(BF16))

    def layer(h, l, tiles):
        tg = h.shape[0]
        bg = tg // seq
        gp = jnp.zeros((8, LANES), F32)
        gp = gp.at[0, 2 * C_HEADS:4 * C_HEADS].set(A_log[l].reshape(-1))
        gp = gp.at[1, 2 * C_HEADS:4 * C_HEADS].set(dt_bias[l].reshape(-1))
        qg = jnp.tile(q_norm_g[l], LANES // HEAD_DIM).reshape(1, LANES)
        kg = jnp.tile(k_norm_g[l], LANES // HEAD_DIM).reshape(1, LANES)
        qpad, ak, av, bq, bk, bv, cqkv, cz, sm = _projection(h, w_main, w_small, l, cos, sin, qg, kg, seg128, gp, seq)

        ya = _gqa(qpad, ak, av, bg, seq)
        yb = _neighbourhood(bq, bk, bv, *tiles, bg, seq)
        conv_pad = jnp.pad(conv_w[l], ((0, 8 - CONV_K), (0, 0)))
        cq, ck, kb_f, kb_b, vb_f, vb_b, gc_f, gc_b, gcs = _short_conv_norm(cqkv, sm, conv_pad, seg256, seq)
        o_f, o_b = _gated_delta(cq, ck, kb_f, kb_b, vb_f, vb_b, gc_f, gc_b, gcs, bg, seq)

        gn = jnp.tile(gdn_norm_g[l], C_HEADS).reshape(1, C_W)
        h1, h1_bf16, route, tile_counts = _merge(h, ya, yb, o_f, o_b, cz, merge_weights, l, gn, seg256,
                                                 ln1_g[l].reshape(1, d), ln1_b[l].reshape(1, d), wr_hi, wr_lo, rb)

        pos, src, tile_expert, n_valid = _dispatch_plan(route, tile_counts, tg)
        xs = jnp.take(h1_bf16, src, axis=0)
        ys = _expert_mlp(xs, tile_expert, n_valid, w1, w3, w2, l)
        y0 = jnp.take(ys, pos[:, 0], axis=0)
        y1 = jnp.take(ys, pos[:, 1], axis=0)
        return _combine(h1, y0, y1, route, ln2_g[l], ln2_b[l])

    n_groups = SEQUENCE_GROUPS if batch % SEQUENCE_GROUPS == 0 else 1
    h = _input_layernorm(x.reshape(t, d), ln0_g, ln0_b)
    groups = [h[g * (t // n_groups):(g + 1) * (t // n_groups)] for g in range(n_groups)]
    for l in range(depth):
        tiles = _na_bias_tiles(na_rpb[l], rows)
        groups = [layer(hg, l, tiles) for hg in groups]
    return jnp.concatenate(groups, axis=0).reshape(batch, seq, d)
```

```python
import functools
import math

import numpy as np
import jax
import jax.numpy as jnp
from jax import lax
from jax.experimental import pallas as pl
from jax.experimental.pallas import tpu as pltpu

F32 = jnp.float32
BF16 = jnp.bfloat16

D_MODEL = 1024
GRID_W = 64
HEAD_DIM = 64
A_Q_HEADS = 8
A_KV_HEADS = 2
ROPE_THETA = 10000.0
B_HEADS = 4
WIN_R = 8
WIN_C = 16
C_HEADS = 4
CONV_K = 5
N_EXPERTS = 32
N_GROUPS = 8
GROUP_SIZE = N_EXPERTS // N_GROUPS
TOP_K = 2
D_EXPERT = 512
DEPTH = 4
DN_ALPHA = (2 * DEPTH) ** 0.25

A_Q_W = A_Q_HEADS * HEAD_DIM
A_KV_W = A_KV_HEADS * HEAD_DIM
B_W = B_HEADS * HEAD_DIM
C_W = C_HEADS * HEAD_DIM
MAIN_W = A_Q_W + 2 * A_KV_W + 3 * B_W + 4 * C_W
SMALL_W = 4 * C_HEADS

LANES = 128
TOKEN_TILE = 512
ATTN_Q_TILE = 512
NA_ROWS_PER_BLOCK = 8
GDN_CHUNK = 64
GDN_BLOCK = 512
EXPERT_TILE = 256
SEQUENCE_GROUPS = 1
VMEM_LIMIT = 48 * 1024 * 1024
NEG_BIG = -1e30
LOG2_E = math.log2(math.e)
HIGHEST = lax.Precision.HIGHEST


def _cparams(*sem):
    return pltpu.CompilerParams(dimension_semantics=sem, vmem_limit_bytes=VMEM_LIMIT)


def _dot(a, b):
    return jnp.dot(a, b, preferred_element_type=F32)


def _dot_nt(a, b):
    return lax.dot_general(a, b, (((1,), (1,)), ((), ())), preferred_element_type=F32)


def _dot_tn(a, b):
    return lax.dot_general(a, b, (((0,), (0,)), ((), ())), preferred_element_type=F32)


def _dot_exact(a, b):
    return jnp.dot(a, b, preferred_element_type=F32, precision=HIGHEST)


def _seg_sum(y, seg):
    hi = y.astype(BF16)
    lo = (y - hi.astype(F32)).astype(BF16)
    return _dot(hi, seg) + _dot(lo, seg)


def _sigmoid(x):
    return 1.0 / (1.0 + jnp.exp(-x))


def _silu(x):
    return x * _sigmoid(x)


def _layernorm_rows(y, g, b):
    mu = jnp.mean(y, axis=-1, keepdims=True)
    yc = y - mu
    var = jnp.mean(yc * yc, axis=-1, keepdims=True)
    return yc * lax.rsqrt(var + 1e-5) * g + b


def _ln_kernel(x_ref, g_ref, b_ref, o_ref):
    o_ref[...] = _layernorm_rows(x_ref[...], g_ref[...], b_ref[...])


def _input_layernorm(x, g, b):
    t, d = x.shape
    row = pl.BlockSpec((TOKEN_TILE, d), lambda i: (i, 0))
    vec = pl.BlockSpec((1, d), lambda i: (0, 0))
    return pl.pallas_call(
        _ln_kernel, grid=(t // TOKEN_TILE,), in_specs=[row, vec, vec], out_specs=row,
        out_shape=jax.ShapeDtypeStruct((t, d), F32), compiler_params=_cparams("parallel"),
        name="input_layernorm")(x, g.reshape(1, d), b.reshape(1, d))


def _proj_kernel(h_ref, w_ref, ws_ref, cos_ref, sin_ref, qg_ref, kg_ref, seg_ref, gp_ref,
                 q_ref, k_ref, v_ref, bq_ref, bk_ref, bv_ref, c_ref, z_ref, sm_ref):
    x = h_ref[...].astype(BF16)
    tm = x.shape[0]
    lane = lax.broadcasted_iota(jnp.int32, (tm, LANES), 1)
    first_half = (lane % (HEAD_DIM // 2)) < (HEAD_DIM // 4)
    low_head = lane < HEAD_DIM
    cos = cos_ref[...]
    sin = sin_ref[...]
    seg = seg_ref[...]

    def mm(a, b):
        return _dot(x, w_ref[0, :, a:b])

    def norm_rope(y, g):
        ms = _seg_sum(y * y, seg) * (1.0 / HEAD_DIM)
        yn = y * lax.rsqrt(ms + 1e-6) * g
        partner = jnp.where(first_half, pltpu.roll(yn, LANES - HEAD_DIM // 4, 1),
                            pltpu.roll(yn, HEAD_DIM // 4, 1))
        return yn * cos + partner * sin

    for j in range(A_Q_W // LANES):
        y = norm_rope(mm(j * LANES, (j + 1) * LANES), qg_ref[...]) * (LOG2_E * HEAD_DIM ** -0.5)
        yr = pltpu.roll(y, HEAD_DIM, 1)
        kv_head = (2 * j) // (A_Q_HEADS // A_KV_HEADS)
        if kv_head == 0:
            even, odd = jnp.where(low_head, y, 0.0), jnp.where(low_head, yr, 0.0)
        else:
            even, odd = jnp.where(low_head, 0.0, yr), jnp.where(low_head, 0.0, y)
        q_ref[:, (2 * j) * LANES:(2 * j + 1) * LANES] = even.astype(BF16)
        q_ref[:, (2 * j + 1) * LANES:(2 * j + 2) * LANES] = odd.astype(BF16)
    o = A_Q_W
    k_ref[...] = norm_rope(mm(o, o + A_KV_W), kg_ref[...]).astype(BF16)
    o += A_KV_W
    v_ref[:, 0:A_KV_W] = mm(o, o + A_KV_W).astype(BF16)
    v_ref[:, A_KV_W:2 * A_KV_W] = jnp.ones((tm, A_KV_W), BF16)
    o += A_KV_W
    bq_ref[...] = (mm(o, o + B_W) * (HEAD_DIM ** -0.5)).astype(BF16)
    o += B_W
    bk_ref[...] = mm(o, o + B_W).astype(BF16)
    o += B_W
    bv_ref[...] = mm(o, o + B_W).astype(BF16)
    o += B_W
    c_ref[...] = mm(o, o + 3 * C_W)
    o += 3 * C_W
    z_ref[...] = mm(o, o + C_W).astype(BF16)

    s = _dot(x, ws_ref[0])
    a = jnp.exp(gp_ref[0:1, :])
    u = s + gp_ref[1:2, :]
    softplus = jnp.maximum(u, 0.0) + jnp.log(1.0 + jnp.exp(-jnp.abs(u)))
    sm_ref[...] = jnp.where(lane < 2 * C_HEADS, _sigmoid(s), jnp.where(lane < SMALL_W, -a * softplus, 0.0))


def _layer_block(a, layer):
    return pl.BlockSpec((1,) + a.shape[1:], lambda i: (layer,) + (0,) * (a.ndim - 1))


def _projection(h, w_main, w_small, layer, cos, sin, qg, kg, seg, gp, seq):
    t, d = h.shape
    tm = TOKEN_TILE
    nseq = seq // tm

    def rows(w):
        return pl.BlockSpec((tm, w), lambda i: (i, 0))

    def const(shape):
        return pl.BlockSpec(shape, lambda i: (0, 0))

    pos = pl.BlockSpec((tm, LANES), lambda i: (i % nseq, 0))
    out_w = [(2 * A_Q_W, BF16), (A_KV_W, BF16), (2 * A_KV_W, BF16), (B_W, BF16), (B_W, BF16), (B_W, BF16),
             (3 * C_W, F32), (C_W, BF16), (LANES, F32)]
    return pl.pallas_call(
        _proj_kernel, grid=(t // tm,),
        in_specs=[rows(d), _layer_block(w_main, layer), _layer_block(w_small, layer), pos, pos, const((1, LANES)),
                  const((1, LANES)), const((LANES, LANES)), const((8, LANES))],
        out_specs=[rows(w) for w, _ in out_w],
        out_shape=[jax.ShapeDtypeStruct((t, w), dt) for w, dt in out_w],
        compiler_params=_cparams("parallel"), name="projection",
    )(h, w_main, w_small, cos, sin, qg, kg, seg, gp)


def _gqa_kernel(q_ref, k_ref, v1_ref, o_ref):
    tq = q_ref.shape[0]
    lane = lax.broadcasted_iota(jnp.int32, (tq, LANES), 1)
    low_head = lane < HEAD_DIM
    k = k_ref[...]
    v1 = v1_ref[...]
    group = A_Q_HEADS // A_KV_HEADS
    outs = []
    for h in range(A_Q_HEADS):
        s = _dot_nt(q_ref[:, h * LANES:(h + 1) * LANES], k)
        p = jnp.exp2(s - jnp.max(s, axis=-1, keepdims=True)).astype(BF16)
        acc = _dot(p, v1)
        outs.append(acc[:, 0:A_KV_W] / acc[:, A_KV_W:A_KV_W + 1])
    for j in range(A_Q_HEADS // 2):
        even, odd = outs[2 * j], outs[2 * j + 1]
        if (2 * j) // group == 0:
            blk = jnp.where(low_head, even, pltpu.roll(odd, HEAD_DIM, 1))
        else:
            blk = jnp.where(low_head, pltpu.roll(even, HEAD_DIM, 1), odd)
        o_ref[:, j * LANES:(j + 1) * LANES] = blk.astype(BF16)


def _gqa(qpad, k, v1, batch, seq):
    t = qpad.shape[0]
    tq = ATTN_Q_TILE
    nq = seq // tq
    return pl.pallas_call(
        _gqa_kernel, grid=(batch, nq),
        in_specs=[pl.BlockSpec((tq, 2 * A_Q_W), lambda b, i: (b * nq + i, 0)),
                  pl.BlockSpec((seq, A_KV_W), lambda b, i: (b, 0)),
                  pl.BlockSpec((seq, 2 * A_KV_W), lambda b, i: (b, 0))],
        out_specs=pl.BlockSpec((tq, A_Q_W), lambda b, i: (b * nq + i, 0)),
        out_shape=jax.ShapeDtypeStruct((t, A_Q_W), BF16),
        compiler_params=_cparams("parallel", "parallel"), name="gqa_attention",
    )(qpad, k, v1)


def _na_key_row0(qb, rows):
    return min(max(NA_ROWS_PER_BLOCK * qb - WIN_R // 2, 0), rows - 2 * NA_ROWS_PER_BLOCK)


def _na_kernel(which_ref, q_ref, k_ref, v_ref, pair_ref, o_ref, bias_ref, *, rows):
    qb = pl.program_id(0)
    nq = q_ref.shape[0]
    nk = 2 * NA_ROWS_PER_BLOCK * GRID_W

    @pl.when(pl.program_id(1) == 0)
    def _():
        n = NA_ROWS_PER_BLOCK
        for rq in range(n):
            for rp in range(n):
                tile = which_ref[(qb * n + rq) * n + rp]
                for h in range(B_HEADS):
                    bias_ref[h, rq * GRID_W:(rq + 1) * GRID_W, rp * LANES:(rp + 1) * LANES] = pair_ref[h, tile]

    row0 = jnp.clip(NA_ROWS_PER_BLOCK * qb - WIN_R // 2, 0, rows - 2 * NA_ROWS_PER_BLOCK)
    start = pl.multiple_of(row0 * GRID_W, 4 * GRID_W)
    lane = lax.broadcasted_iota(jnp.int32, (nq, LANES), 1)
    low_head = lane < HEAD_DIM
    zero = jnp.zeros((), BF16)
    for j in range(B_W // LANES):
        kblk = k_ref[pl.ds(start, nk), j * LANES:(j + 1) * LANES]
        vblk = v_ref[pl.ds(start, nk), j * LANES:(j + 1) * LANES]
        qblk = q_ref[:, j * LANES:(j + 1) * LANES]
        outs = []
        for par in range(2):
            qh = jnp.where(low_head if par == 0 else jnp.logical_not(low_head), qblk, zero)
            s = _dot_nt(qh, kblk) + bias_ref[2 * j + par]
            m = jnp.max(s, axis=-1, keepdims=True)
            p = jnp.exp(s - m)
            l = jnp.sum(p, axis=-1, keepdims=True)
            outs.append(_dot(p.astype(BF16), vblk) / l)
        o_ref[:, j * LANES:(j + 1) * LANES] = jnp.where(low_head, outs[0], outs[1]).astype(BF16)


def _na_bias_tiles(rpb, rows):
    nqb = rows // NA_ROWS_PER_BLOCK
    nkr = 2 * NA_ROWS_PER_BLOCK
    wr = min(WIN_R, rows)
    n_dr = 2 * WIN_R - 1
    c = np.arange(GRID_W)
    c0 = np.clip(c - WIN_C // 2, 0, GRID_W - WIN_C)
    in_c = (c[None, :] >= c0[:, None]) & (c[None, :] < c0[:, None] + WIN_C)
    dc = np.clip(c[None, :] - c[:, None] + (WIN_C - 1), 0, 2 * WIN_C - 2)
    col_tiles = jnp.where(jnp.asarray(in_c)[None, None], rpb[:, :, jnp.asarray(dc)], NEG_BIG)
    col_tiles = jnp.concatenate([col_tiles, jnp.full((B_HEADS, 1, GRID_W, GRID_W), NEG_BIG, F32)], axis=1)
    tile_idx = np.full((nqb, NA_ROWS_PER_BLOCK, nkr), n_dr, np.int32)
    for qb in range(nqb):
        for rq in range(NA_ROWS_PER_BLOCK):
            r = NA_ROWS_PER_BLOCK * qb + rq
            r0 = min(max(r - wr // 2, 0), rows - wr)
            for rk in range(nkr):
                kr = _na_key_row0(qb, rows) + rk
                if r0 <= kr < r0 + wr:
                    tile_idx[qb, rq, rk] = kr - r + (WIN_R - 1)
    pairs, which = np.unique(tile_idx.reshape(-1, 2), axis=0, return_inverse=True)
    pair_tiles = jnp.concatenate([jnp.take(col_tiles, jnp.asarray(pairs[:, 0]), axis=1),
                                  jnp.take(col_tiles, jnp.asarray(pairs[:, 1]), axis=1)], axis=-1)
    return pair_tiles, jnp.asarray(which.reshape(-1).astype(np.int32))


def _neighbourhood(q, k, v, pair_tiles, which, batch, seq):
    t = q.shape[0]
    rows = seq // GRID_W
    nqb = rows // NA_ROWS_PER_BLOCK
    nq = NA_ROWS_PER_BLOCK * GRID_W
    kv = pl.BlockSpec((seq, B_W), lambda qb, b, w: (b, 0))
    qo = pl.BlockSpec((nq, B_W), lambda qb, b, w: (b * nqb + qb, 0))
    grid_spec = pltpu.PrefetchScalarGridSpec(
        num_scalar_prefetch=1, grid=(nqb, batch),
        in_specs=[qo, kv, kv, pl.BlockSpec(pair_tiles.shape, lambda qb, b, w: (0, 0, 0, 0))],
        out_specs=qo, scratch_shapes=[pltpu.VMEM((B_HEADS, nq, 2 * nq), F32)])
    return pl.pallas_call(
        functools.partial(_na_kernel, rows=rows), grid_spec=grid_spec,
        out_shape=jax.ShapeDtypeStruct((t, B_W), BF16),
        compiler_params=_cparams("arbitrary", "arbitrary"), name="neighbourhood_attention",
    )(which, q, k, v, pair_tiles)


def _dot_select(x, sel):
    hi = x.astype(BF16)
    r1 = x - hi.astype(F32)
    mid = r1.astype(BF16)
    lo = (r1 - mid.astype(F32)).astype(BF16)
    return _dot(hi, sel) + _dot(mid, sel) + _dot(lo, sel)


def _conv_kernel(c_ref, prev_ref, next_ref, w_ref, seg_ref, sm_ref, e_ref, cum_f_ref, cum_b_ref,
                 q_ref, k_ref, kbf_ref, kbb_ref, vbf_ref, vbb_ref, gcf_ref, gcb_ref, gcs_ref, pad_ref, *,
                 tiles_per_seq):
    i = pl.program_id(0)
    tm = c_ref.shape[0]
    halo = prev_ref.shape[0]
    first = (i % tiles_per_seq) == 0
    last = (i % tiles_per_seq) == tiles_per_seq - 1
    pad_ref[0:halo, :] = jnp.where(first, 0.0, prev_ref[...])
    pad_ref[halo:halo + tm, :] = c_ref[...]
    pad_ref[halo + tm:2 * halo + tm, :] = jnp.where(last, 0.0, next_ref[...])
    acc = None
    for j in range(CONV_K):
        o = halo - CONV_K // 2 + j
        term = w_ref[j:j + 1, :] * pad_ref[o:o + tm, :]
        acc = term if acc is None else acc + term
    y = _silu(acc)
    seg = seg_ref[...]

    def l2n(u):
        return u * lax.rsqrt(_seg_sum(u * u, seg) + 1e-6)

    q_ref[...] = (l2n(y[:, 0:C_W]) * (HEAD_DIM ** -0.5)).astype(BF16)
    k = l2n(y[:, C_W:2 * C_W])
    v = y[:, 2 * C_W:3 * C_W]
    k_ref[...] = k.astype(BF16)

    sm = sm_ref[...]
    beta_f = _dot_select(sm, e_ref[:, 0 * C_W:1 * C_W])
    beta_b = _dot_select(sm, e_ref[:, 1 * C_W:2 * C_W])
    kbf_ref[...] = (k * beta_f).astype(BF16)
    kbb_ref[...] = (k * beta_b).astype(BF16)
    vbf_ref[...] = (v * beta_f).astype(BF16)
    vbb_ref[...] = (v * beta_b).astype(BF16)
    cum_f = _dot_select_lhs(cum_f_ref[...], sm)
    cum_b = _dot_select_lhs(cum_b_ref[...], sm)
    gcf_ref[...] = _dot_select(cum_f, e_ref[:, 2 * C_W:3 * C_W])
    gcb_ref[...] = _dot_select(cum_b, e_ref[:, 3 * C_W:4 * C_W])
    lane = lax.broadcasted_iota(jnp.int32, sm.shape, 1)
    gcs_ref[...] = jnp.where(lane < 3 * C_HEADS, cum_f, cum_b)


def _dot_select_lhs(sel, x):
    hi = x.astype(BF16)
    r1 = x - hi.astype(F32)
    mid = r1.astype(BF16)
    lo = (r1 - mid.astype(F32)).astype(BF16)
    return _dot(sel, hi) + _dot(sel, mid) + _dot(sel, lo)


def _short_conv_norm(c, sm, conv_w, seg, seq):
    t, w = c.shape
    tm = TOKEN_TILE
    halo = 8
    nb = tm // halo
    last_blk = t // halo - 1
    e, cum_f, cum_b = _gdn_constants(tm)
    out = pl.BlockSpec((tm, C_W), lambda i: (i, 0))

    def const(a):
        return pl.BlockSpec(a.shape, lambda i: (0, 0))

    return pl.pallas_call(
        functools.partial(_conv_kernel, tiles_per_seq=seq // tm), grid=(t // tm,),
        in_specs=[pl.BlockSpec((tm, w), lambda i: (i, 0)),
                  pl.BlockSpec((halo, w), lambda i: (jnp.maximum(i * nb - 1, 0), 0)),
                  pl.BlockSpec((halo, w), lambda i: (jnp.minimum((i + 1) * nb, last_blk), 0)),
                  pl.BlockSpec((8, w), lambda i: (0, 0)),
                  pl.BlockSpec((C_W, C_W), lambda i: (0, 0)),
                  pl.BlockSpec((tm, LANES), lambda i: (i, 0)), const(e), const(cum_f), const(cum_b)],
        out_specs=[out] * 8 + [pl.BlockSpec((tm, LANES), lambda i: (i, 0))],
        out_shape=([jax.ShapeDtypeStruct((t, C_W), BF16)] * 6 + [jax.ShapeDtypeStruct((t, C_W), F32)] * 2
                   + [jax.ShapeDtypeStruct((t, LANES), F32)]),
        scratch_shapes=[pltpu.VMEM((tm + 2 * halo, w), F32)],
        compiler_params=_cparams("parallel"), name="gdn_conv_norm",
    )(c, c, c, conv_w, seg, sm, e, cum_f, cum_b)


def _block_diag(x, same_head):
    return jnp.where(same_head, jnp.concatenate([x] * C_HEADS, axis=0), 0.0)


def _gdn_prepare(chains):
    C = GDN_CHUNK
    ri = lax.broadcasted_iota(jnp.int32, (C, C_W), 0)
    lj = lax.broadcasted_iota(jnp.int32, (C, C_W), 1) % C
    rr = lax.broadcasted_iota(jnp.int32, (C_HEADS * C, C_W), 0) // C
    ll = lax.broadcasted_iota(jnp.int32, (C_HEADS * C, C_W), 1) // HEAD_DIM
    same_head = rr == ll
    eye = jnp.where(ri == lj, 1.0, 0.0)

    def bd(x):
        return _block_diag(x.astype(BF16), same_head)

    n = range(len(chains))
    decay = [jnp.exp(jnp.where((ri <= lj) if ch["reverse"] else (ri >= lj), ch["gc"] - ch["grow"], -jnp.inf))
             for ch in chains]
    r = [_dot_nt(jnp.concatenate([ch["kb"], ch["q"]], axis=0), bd(ch["k"])) for ch in chains]
    m = [jnp.where((ri < lj) if chains[i]["reverse"] else (ri > lj), r[i][:C] * decay[i], 0.0) for i in n]
    qk = [(r[i][C:] * decay[i]).astype(BF16) for i in n]

    p = [eye - m[i] for i in n]
    a = [_dot(m[i].astype(BF16), bd(m[i])) for i in n]
    for _ in range(int(math.log2(C)) - 2):
        r2 = [_dot(jnp.concatenate([a[i], p[i]], axis=0).astype(BF16), bd(a[i])) for i in n]
        a = [r2[i][:C] for i in n]
        p = [p[i] + r2[i][C:] for i in n]
    t_inv = [(p[i] + _dot(p[i].astype(BF16), bd(a[i]))).astype(BF16) for i in n]

    egc = [jnp.exp(ch["gc"]) for ch in chains]
    uw = [_dot(t_inv[i], jnp.concatenate([bd(chains[i]["vb"]), bd(chains[i]["kb"].astype(F32) * egc[i])], axis=1))
          for i in n]
    out = []
    for i in n:
        ch = chains[i]
        gc = ch["gc"]
        g_last = gc[0:1, :] if ch["reverse"] else gc[C - 1:C, :]
        ke = (ch["k"].astype(F32) * jnp.exp(g_last - gc)).astype(BF16)
        ab = _dot_tn(ke, uw[i].astype(BF16))
        u, w = uw[i][:, :C_W], uw[i][:, C_W:]
        qo = _dot(qk[i], jnp.concatenate([bd(u), bd(w)], axis=1))
        out.append(dict(
            a=jnp.where(same_head, -ab[:, C_W:], 0.0).astype(BF16),
            b=jnp.where(same_head, ab[:, :C_W], 0.0),
            d=jnp.exp(g_last),
            qeff=(ch["q"].astype(F32) * egc[i] - qo[:, C_W:]).astype(BF16),
            oloc=qo[:, :C_W]))
    return out


GDN_PREP_CHUNKS = 2


def _gdn_kernel(qf, kf, kbf, vbf, gcf, gwf, qb, kb, kbb, vbb, gcb, gwb, of_ref, ob_ref,
                state_ref, a_ref, b_ref, d_ref, qeff_ref, oloc_ref):
    C = GDN_CHUNK
    n = GDN_BLOCK // C
    dirs = ((qf, kf, kbf, vbf, gcf, gwf, False), (qb, kb, kbb, vbb, gcb, gwb, True))

    @pl.when(pl.program_id(1) == 0)
    def _():
        state_ref[...] = jnp.zeros_like(state_ref)

    def prepare(it, carry):
        chains, where = [], []
        for d, (q, k, kbeta, vbeta, gc, gw, reverse) in enumerate(dirs):
            for j in range(GDN_PREP_CHUNKS):
                c = it * GDN_PREP_CHUNKS + j
                sl = pl.ds(pl.multiple_of(c * C, C), C)
                chains.append(dict(q=q[sl, :], k=k[sl, :], kb=kbeta[sl, :], vb=vbeta[sl, :], gc=gc[sl, :],
                                   grow=gw[0, pl.ds(c, 1), :], reverse=reverse))
                where.append((d, c))
        for (d, c), res in zip(where, _gdn_prepare(chains)):
            a_ref[d, c] = res["a"]
            b_ref[d, c] = res["b"]
            d_ref[d, c] = res["d"]
            qeff_ref[d, c] = res["qeff"]
            oloc_ref[d, c] = res["oloc"]
        return carry

    lax.fori_loop(0, n // GDN_PREP_CHUNKS, prepare, 0)

    def scan(step, carry):
        cs = (step, n - 1 - step)
        states = [state_ref[d] for d in range(2)]
        lhs = [jnp.concatenate([a_ref[d, cs[d]], qeff_ref[d, cs[d]]], axis=0) for d in range(2)]
        res = [_dot(lhs[d], states[d].astype(BF16)) for d in range(2)]
        for d, o_ref in enumerate((of_ref, ob_ref)):
            state_ref[d] = states[d] * d_ref[d, cs[d]] + res[d][:C_W] + b_ref[d, cs[d]]
            o_ref[pl.ds(pl.multiple_of(cs[d] * C, C), C), :] = res[d][C_W:] + oloc_ref[d, cs[d]]
        return carry

    lax.fori_loop(0, n, scan, 0)


def _gdn_constants(tile):
    C = GDN_CHUNK
    e = np.zeros((LANES, 4 * C_W), np.float32)
    for kind in range(4):
        for h in range(C_HEADS):
            e[kind * C_HEADS + h, kind * C_W + h * HEAD_DIM:kind * C_W + (h + 1) * HEAD_DIM] = 1.0
    i = np.arange(C)
    tril = (i[None, :] <= i[:, None]).astype(np.float32)
    cum_f = np.kron(np.eye(tile // C, dtype=np.float32), tril)
    cum_b = np.kron(np.eye(tile // C, dtype=np.float32), tril.T)
    return tuple(jnp.asarray(a, BF16) for a in (e, cum_f, cum_b))


def _gated_delta(q, k, kb_f, kb_b, vb_f, vb_b, gc_f, gc_b, gcs, batch, seq):
    t = q.shape[0]
    nb = seq // GDN_BLOCK
    nc = GDN_BLOCK // GDN_CHUNK

    def rows(col0):
        g = gcs[:, col0:col0 + C_HEADS].reshape(t // GDN_CHUNK, GDN_CHUNK, C_HEADS)
        return jnp.transpose(g, (0, 2, 1)).reshape(t // GDN_BLOCK, nc, C_W)

    def fwd(w):
        return pl.BlockSpec((GDN_BLOCK, w), lambda b, s: (b * nb + s, 0))

    def bwd(w):
        return pl.BlockSpec((GDN_BLOCK, w), lambda b, s: (b * nb + nb - 1 - s, 0))

    gf_spec = pl.BlockSpec((1, nc, C_W), lambda b, s: (b * nb + s, 0, 0))
    gb_spec = pl.BlockSpec((1, nc, C_W), lambda b, s: (b * nb + nb - 1 - s, 0, 0))
    return pl.pallas_call(
        _gdn_kernel, grid=(batch, nb),
        in_specs=[fwd(C_W)] * 5 + [gf_spec] + [bwd(C_W)] * 5 + [gb_spec],
        out_specs=[fwd(C_W), bwd(C_W)],
        out_shape=[jax.ShapeDtypeStruct((t, C_W), F32)] * 2,
        scratch_shapes=[pltpu.VMEM((2, C_W, C_W), F32),
                        pltpu.VMEM((2, nc, C_W, C_W), BF16), pltpu.VMEM((2, nc, C_W, C_W), F32),
                        pltpu.VMEM((2, nc, 1, C_W), F32),
                        pltpu.VMEM((2, nc, GDN_CHUNK, C_W), BF16), pltpu.VMEM((2, nc, GDN_CHUNK, C_W), F32)],
        compiler_params=_cparams("parallel", "arbitrary"), name="gated_delta_rule",
    )(q, k, kb_f, vb_f, gc_f, rows(2 * C_HEADS), q, k, kb_b, vb_b, gc_b, rows(3 * C_HEADS))


def _merge_kernel(h_ref, ya_ref, yb_ref, of_ref, ob_ref, z_ref, wg_ref, wa_ref, wb_ref, wc_ref, wo_ref,
                  gn_ref, seg_ref, lg_ref, lb_ref, wrh_ref, wrl_ref, rb_ref, tril_ref,
                  h1_ref, h1b_ref, route_ref, count_ref):
    h = h_ref[...]
    x = h.astype(BF16)
    tm = h.shape[0]
    d = D_MODEL
    o = of_ref[...] + ob_ref[...]
    ms = _seg_sum(o * o, seg_ref[...]) * (1.0 / HEAD_DIM)
    yc = o * lax.rsqrt(ms + 1e-6) * gn_ref[...] * _silu(z_ref[...].astype(F32))
    mix = _sigmoid(_dot(x, wg_ref[0, :, 0:d])) * _dot(ya_ref[...], wa_ref[0])
    mix = mix + _sigmoid(_dot(x, wg_ref[0, :, d:2 * d])) * _dot(yb_ref[...], wb_ref[0])
    mix = mix + _sigmoid(_dot(x, wg_ref[0, :, 2 * d:3 * d])) * _dot(yc.astype(BF16), wc_ref[0])
    y = DN_ALPHA * h + _dot(mix.astype(BF16), wo_ref[0])
    h1 = _layernorm_rows(y, lg_ref[...], lb_ref[...])
    h1_ref[...] = h1
    h1b_ref[...] = h1.astype(BF16)

    hi = h1.astype(BF16)
    lo = (h1 - hi.astype(F32)).astype(BF16)
    logits = _dot(hi, wrh_ref[...]) + _dot(lo, wrh_ref[...]) + _dot(hi, wrl_ref[...])
    scores = _sigmoid(logits)
    sel = scores + rb_ref[...]
    lane = lax.broadcasted_iota(jnp.int32, (tm, LANES), 1)
    in_grp = lane < N_GROUPS
    sel_m = [sel if m == 0 else pltpu.roll(sel, LANES - m * N_GROUPS, 1) for m in range(GROUP_SIZE)]
    sc_m = [scores if m == 0 else pltpu.roll(scores, LANES - m * N_GROUPS, 1) for m in range(GROUP_SIZE)]
    hi01, lo01 = jnp.maximum(sel_m[0], sel_m[1]), jnp.minimum(sel_m[0], sel_m[1])
    hi23, lo23 = jnp.maximum(sel_m[2], sel_m[3]), jnp.minimum(sel_m[2], sel_m[3])
    top1 = jnp.maximum(hi01, hi23)
    top2 = jnp.maximum(jnp.minimum(hi01, hi23), jnp.maximum(lo01, lo23))
    grp_score = jnp.where(in_grp, top1 + top2, -jnp.inf)
    best_score = jnp.max(grp_score, axis=-1, keepdims=True)
    best = jnp.min(jnp.where(grp_score == best_score, lane, LANES), axis=-1, keepdims=True)
    pick = lane == best
    val = [jnp.sum(jnp.where(pick, s, 0.0), axis=-1, keepdims=True) for s in sel_m]
    aff = [jnp.sum(jnp.where(pick, s, 0.0), axis=-1, keepdims=True) for s in sc_m]
    e_sel = [jnp.zeros((tm, 1), F32)] * TOP_K
    w_sel = [jnp.zeros((tm, 1), F32)] * TOP_K
    for m in range(GROUP_SIZE):
        rank = jnp.zeros((tm, 1), jnp.int32)
        for j in range(GROUP_SIZE):
            if j != m:
                ahead = (val[j] > val[m]) | ((val[j] == val[m]) & (j < m))
                rank = rank + ahead.astype(jnp.int32)
        for kk in range(TOP_K):
            e_sel[kk] = jnp.where(rank == kk, (best * GROUP_SIZE + m).astype(F32), e_sel[kk])
            w_sel[kk] = jnp.where(rank == kk, aff[m], w_sel[kk])
    tot = w_sel[0] + w_sel[1]

    lane_f = lane.astype(F32)
    chosen = [lane_f == e_sel[kk] for kk in range(TOP_K)]
    onehot = jnp.where(chosen[0] | chosen[1], 1.0, 0.0).astype(BF16)
    running = _dot(tril_ref[...], onehot)
    rank = [jnp.sum(jnp.where(chosen[kk], running, 0.0), axis=-1, keepdims=True) - 1.0 for kk in range(TOP_K)]
    count_ref[0] = running[tm - 1:tm, :]
    route_ref[...] = jnp.where(lane == 0, e_sel[0], jnp.where(lane == 1, e_sel[1], jnp.where(
        lane == 2, w_sel[0] / tot, jnp.where(lane == 3, w_sel[1] / tot, jnp.where(
            lane == 4, rank[0], jnp.where(lane == 5, rank[1], 0.0))))))


def _merge(h, ya, yb, o_f, o_b, z, layer_weights, layer, gn, seg, lg, lb, wr_hi, wr_lo, rb):
    t, d = h.shape
    tm = TOKEN_TILE

    def rows(w):
        return pl.BlockSpec((tm, w), lambda i: (i, 0))

    def const(a):
        return pl.BlockSpec(a.shape, lambda i: (0, 0))

    i = np.arange(tm)
    tril = jnp.asarray((i[None, :] <= i[:, None]).astype(np.float32), BF16)
    consts = (gn, seg, lg, lb, wr_hi, wr_lo, rb, tril)
    return pl.pallas_call(
        _merge_kernel, grid=(t // tm,),
        in_specs=([rows(d), rows(A_Q_W), rows(B_W), rows(C_W), rows(C_W), rows(C_W)]
                  + [_layer_block(w, layer) for w in layer_weights] + [const(a) for a in consts]),
        out_specs=[rows(d), rows(d), rows(LANES), pl.BlockSpec((1, 1, LANES), lambda i: (i, 0, 0))],
        out_shape=[jax.ShapeDtypeStruct((t, d), F32), jax.ShapeDtypeStruct((t, d), BF16),
                   jax.ShapeDtypeStruct((t, LANES), F32), jax.ShapeDtypeStruct((t // tm, 1, LANES), F32)],
        compiler_params=_cparams("parallel"), name="merge_outproj_ln_router",
    )(h, ya, yb, o_f, o_b, z, *layer_weights, *consts)


def _expert_kernel(te_ref, nv_ref, x_ref, w1_ref, w3_ref, w2_ref, y_ref, w1b_ref, w3b_ref, w2b_ref):
    i = pl.program_id(0)

    @pl.when((i == 0) | (te_ref[i] != te_ref[jnp.maximum(i - 1, 0)]))
    def _():
        w1b_ref[...] = w1_ref[0, 0].astype(BF16)
        w3b_ref[...] = w3_ref[0, 0].astype(BF16)
        w2b_ref[...] = w2_ref[0, 0].astype(BF16)

    @pl.when(i < nv_ref[0])
    def _():
        x = x_ref[...]
        hid = _silu(_dot(x, w1b_ref[...])) * _dot(x, w3b_ref[...])
        y_ref[...] = _dot(hid.astype(BF16), w2b_ref[...]).astype(BF16)

    @pl.when(i >= nv_ref[0])
    def _():
        y_ref[...] = jnp.zeros_like(y_ref)


def _expert_mlp(xs, tile_expert, n_valid, w1, w3, w2, layer):
    mp, d = xs.shape
    te = EXPERT_TILE
    grid_spec = pltpu.PrefetchScalarGridSpec(
        num_scalar_prefetch=2, grid=(mp // te,),
        in_specs=[pl.BlockSpec((te, d), lambda i, e, n: (i, 0)),
                  pl.BlockSpec((1, 1, d, D_EXPERT), lambda i, e, n: (layer, e[i], 0, 0)),
                  pl.BlockSpec((1, 1, d, D_EXPERT), lambda i, e, n: (layer, e[i], 0, 0)),
                  pl.BlockSpec((1, 1, D_EXPERT, d), lambda i, e, n: (layer, e[i], 0, 0))],
        out_specs=pl.BlockSpec((te, d), lambda i, e, n: (i, 0)),
        scratch_shapes=[pltpu.VMEM((d, D_EXPERT), BF16), pltpu.VMEM((d, D_EXPERT), BF16),
                        pltpu.VMEM((D_EXPERT, d), BF16)])
    return pl.pallas_call(
        _expert_kernel, grid_spec=grid_spec, out_shape=jax.ShapeDtypeStruct((mp, d), BF16),
        compiler_params=_cparams("arbitrary"), name="expert_mlp",
    )(tile_expert, n_valid, xs, w1, w3, w2)


def _dispatch_plan(route, tile_counts, t):
    te = EXPERT_TILE
    e = route[:, 0:TOP_K].astype(jnp.int32)
    rank = route[:, 2 * TOP_K:3 * TOP_K].astype(jnp.int32)
    tile_counts = tile_counts[:, 0, :N_EXPERTS].astype(jnp.int32)
    before = jnp.cumsum(tile_counts, axis=0) - tile_counts
    counts = tile_counts.sum(axis=0)
    padded = ((counts + te - 1) // te) * te
    ends = jnp.cumsum(padded)
    base = (ends - padded)[None, :] + before
    base_tok = jnp.repeat(base, t // tile_counts.shape[0], axis=0)
    chosen = e[:, :, None] == jnp.arange(N_EXPERTS, dtype=jnp.int32)[None, None, :]
    pos = jnp.sum(jnp.where(chosen, base_tok[:, None, :], 0), axis=-1) + rank
    n_tiles = (t * TOP_K) // te + N_EXPERTS
    tile_start = jnp.arange(n_tiles, dtype=jnp.int32) * te
    tile_expert = jnp.minimum((tile_start[:, None] >= ends[None, :]).astype(jnp.int32).sum(axis=1), N_EXPERTS - 1)
    n_valid = (ends[-1] // te).astype(jnp.int32).reshape(1)
    src = jnp.zeros((n_tiles * te,), jnp.int32).at[pos.reshape(-1)].set(
        jnp.repeat(jnp.arange(t, dtype=jnp.int32), TOP_K))
    return pos, src, tile_expert, n_valid


def _combine_kernel(h_ref, y0_ref, y1_ref, route_ref, g_ref, b_ref, o_ref):
    lane = lax.broadcasted_iota(jnp.int32, route_ref.shape, 1)
    r = route_ref[...]
    w0 = jnp.sum(jnp.where(lane == 2, r, 0.0), axis=-1, keepdims=True)
    w1 = jnp.sum(jnp.where(lane == 3, r, 0.0), axis=-1, keepdims=True)
    y = w0 * y0_ref[...].astype(F32) + w1 * y1_ref[...].astype(F32)
    o_ref[...] = _layernorm_rows(DN_ALPHA * h_ref[...] + y, g_ref[...], b_ref[...])


def _combine(h1, y0, y1, route, g, b):
    t, d = h1.shape
    row = pl.BlockSpec((TOKEN_TILE, d), lambda i: (i, 0))
    vec = pl.BlockSpec((1, d), lambda i: (0, 0))
    return pl.pallas_call(
        _combine_kernel, grid=(t // TOKEN_TILE,),
        in_specs=[row, row, row, pl.BlockSpec((TOKEN_TILE, LANES), lambda i: (i, 0)), vec, vec],
        out_specs=row, out_shape=jax.ShapeDtypeStruct((t, d), F32),
        compiler_params=_cparams("parallel"), name="combine_ln",
    )(h1, y0, y1, route, g.reshape(1, d), b.reshape(1, d))


def _rope_tables(seq):
    tkn = np.arange(seq)
    pos = np.stack([tkn // GRID_W, tkn % GRID_W], axis=1).astype(np.float32)
    half = HEAD_DIM // 2
    inv = ROPE_THETA ** (-np.arange(0, half, 2, dtype=np.float32) / half)
    d = np.arange(LANES) % HEAD_DIM
    axis = d // half
    r = d % half
    ang = pos[:, axis] * inv[r % (half // 2)][None, :]
    sign = np.where(r < half // 2, -1.0, 1.0)[None, :]
    return jnp.asarray(np.cos(ang), F32), jnp.asarray(np.sin(ang) * sign, F32)


def _segment_matrix(width):
    i = np.arange(width) // HEAD_DIM
    return jnp.asarray((i[:, None] == i[None, :]).astype(np.float32), BF16)


def _router_layout(w_router, router_bias):
    perm = np.array([(l % N_GROUPS) * GROUP_SIZE + l // N_GROUPS for l in range(N_EXPERTS)])
    w = jnp.pad(w_router[:, perm], ((0, 0), (0, LANES - N_EXPERTS)))
    hi = w.astype(BF16)
    lo = (w - hi.astype(F32)).astype(BF16)
    rb = jnp.pad(router_bias[perm], (0, LANES - N_EXPERTS)).reshape(1, LANES)
    return hi, lo, rb


def kernel(x, ln0_g, ln0_b, w_in, q_norm_g, k_norm_g, na_rpb, conv_w, A_log, dt_bias, gdn_norm_g,
           w_branch_a, w_branch_b, w_branch_c, w_out, ln1_g, ln1_b, w_router, router_bias, w1, w3, w2,
           ln2_g, ln2_b):
    batch, seq, d = x.shape
    t = batch * seq
    depth = w_in.shape[0]
    rows = seq // GRID_W
    cos, sin = _rope_tables(seq)
    seg128 = _segment_matrix(LANES)
    seg256 = _segment_matrix(C_W)
    wr_hi, wr_lo, rb = _router_layout(w_router, router_bias)

    w_main = w_in[:, :, :MAIN_W].astype(BF16)
    w_small = jnp.pad(w_in[:, :, MAIN_W:MAIN_W + SMALL_W], ((0, 0), (0, 0), (0, LANES - SMALL_W))).astype(BF16)
    merge_weights = (w_in[:, :, MAIN_W + SMALL_W:].astype(BF16), w_branch_a.astype(BF16),
                     w_branch_b.astype(BF16), w_branch_c.astype(BF16), w_out.astype(BF16))

    def layer(h, l, tiles):
        tg = h.shape[0]
        bg = tg // seq
        gp = jnp.zeros((8, LANES), F32)
        gp = gp.at[0, 2 * C_HEADS:4 * C_HEADS].set(A_log[l].reshape(-1))
        gp = gp.at[1, 2 * C_HEADS:4 * C_HEADS].set(dt_bias[l].reshape(-1))
        qg = jnp.tile(q_norm_g[l], LANES // HEAD_DIM).reshape(1, LANES)
        kg = jnp.tile(k_norm_g[l], LANES // HEAD_DIM).reshape(1, LANES)
        qpad, ak, av, bq, bk, bv, cqkv, cz, sm = _projection(h, w_main, w_small, l, cos, sin, qg, kg, seg128, gp, seq)

        ya = _gqa(qpad, ak, av, bg, seq)
        yb = _neighbourhood(bq, bk, bv, *tiles, bg, seq)
        conv_pad = jnp.pad(conv_w[l], ((0, 8 - CONV_K), (0, 0)))
        cq, ck, kb_f, kb_b, vb_f, vb_b, gc_f, gc_b, gcs = _short_conv_norm(cqkv, sm, conv_pad, seg256, seq)
        o_f, o_b = _gated_delta(cq, ck, kb_f, kb_b, vb_f, vb_b, gc_f, gc_b, gcs, bg, seq)

        gn = jnp.tile(gdn_norm_g[l], C_HEADS).reshape(1, C_W)
        h1, h1_bf16, route, tile_counts = _merge(h, ya, yb, o_f, o_b, cz, merge_weights, l, gn, seg256,
                                                 ln1_g[l].reshape(1, d), ln1_b[l].reshape(1, d), wr_hi, wr_lo, rb)

        pos, src, tile_expert, n_valid = _dispatch_plan(route, tile_counts, tg)
        xs = jnp.take(h1_bf16, src, axis=0)
        ys = _expert_mlp(xs, tile_expert, n_valid, w1, w3, w2, l)
        y0 = jnp.take(ys, pos[:, 0], axis=0)
        y1 = jnp.take(ys, pos[:, 1], axis=0)
        return _combine(h1, y0, y1, route, ln2_g[l], ln2_b[l])

    n_groups = SEQUENCE_GROUPS if batch % SEQUENCE_GROUPS == 0 else 1
    h = _input_layernorm(x.reshape(t, d), ln0_g, ln0_b)
    groups = [h[g * (t // n_groups):(g + 1) * (t // n_groups)] for g in range(n_groups)]
    for l in range(depth):
        tiles = _na_bias_tiles(na_rpb[l], rows)
        groups = [layer(hg, l, tiles) for hg in groups]
    return jnp.concatenate(groups, axis=0).reshape(batch, seq, d)
```
